```python
import math
import jax, jax.numpy as jnp
from jax import lax
import numpy as np

D_MODEL = 2048
BATCH = 1
SEQ = 16384
DEPTH = 1

CHUNK = 64
PLE_DIM = 256
D_FF = 5632
MIX_WIDTH = D_MODEL
POOL_WIDTH = MIX_WIDTH // 2
POOL_GROUPS = 4
POOL_GROUP_DIM = POOL_WIDTH // POOL_GROUPS
POOL_WINDOWS = (2, 4, 8, 16)
N_HEADS = 8
HEAD_DIM = 128
ATTN_WIDTH = N_HEADS * HEAD_DIM
LEFT_CHUNKS = 8
LEFT = LEFT_CHUNKS * CHUNK
BAND = (LEFT_CHUNKS + 1) * CHUNK
REL_CLIP = 256
IN_COLS = POOL_WIDTH + 3 * ATTN_WIDTH
N_BRANCHES = 2
EPS = 1e-6
MASK_VALUE = -1e30

kernel_name = "hybrid_pool_chunkattn_macaron_block"


def rms_norm(x, g):
    xf = x.astype(jnp.float32)
    y = xf * lax.rsqrt(jnp.mean(xf * xf, axis=-1, keepdims=True) + EPS)
    return (y * g.astype(jnp.float32)).astype(x.dtype)


def swiglu(x, w_gate, w_up, w_down):
    return (jax.nn.silu(x @ w_gate) * (x @ w_up)) @ w_down


def pool_mixer(z, group_w, scale):
    b, s, _ = z.shape
    zg = z.reshape(b, s, POOL_GROUPS, POOL_GROUP_DIM)
    zf = zg.astype(jnp.float32)
    cs = jnp.cumsum(zf, axis=1)
    cs_pad = jnp.pad(cs, ((0, 0), (1, 0), (0, 0), (0, 0)))
    t = jnp.arange(s, dtype=jnp.int32)
    pooled = []
    for g, w in enumerate(POOL_WINDOWS):
        upper = cs_pad[:, 1:, g]
        lower = jnp.pad(cs_pad[:, : s - w + 1, g], ((0, 0), (w - 1, 0), (0, 0)))
        count = jnp.minimum(t + 1, w).astype(jnp.float32)[None, :, None]
        pooled.append((upper - lower) / count)
    pooled = jnp.stack(pooled, axis=2)
    diff = (pooled - zf).astype(z.dtype)
    y = jnp.einsum('bsgc,gcd->bsgd', diff, group_w)
    return y.reshape(b, s, POOL_WIDTH) * scale


def chunk_attention(q, k, v, q_gain, k_gain, rel_bias):
    b, s, _ = q.shape
    nc = s // CHUNK
    q = rms_norm(q.reshape(b, s, N_HEADS, HEAD_DIM), q_gain) * (HEAD_DIM ** -0.5)
    k = rms_norm(k.reshape(b, s, N_HEADS, HEAD_DIM), k_gain)
    v = v.reshape(b, s, N_HEADS, HEAD_DIM)
    q = q.transpose(0, 2, 1, 3).reshape(b, N_HEADS, nc, CHUNK, HEAD_DIM)
    q_chunks = jnp.moveaxis(q, 2, 0)
    k_pad = jnp.pad(k.transpose(0, 2, 1, 3), ((0, 0), (0, 0), (LEFT, 0), (0, 0)))
    v_pad = jnp.pad(v.transpose(0, 2, 1, 3), ((0, 0), (0, 0), (LEFT, 0), (0, 0)))
    qi = jnp.arange(CHUNK, dtype=jnp.int32)[:, None]
    kj = jnp.arange(BAND, dtype=jnp.int32)[None, :]
    rel_idx = jnp.clip(qi - kj + LEFT, -REL_CLIP, REL_CLIP) + REL_CLIP
    bias = rel_bias[:, rel_idx].astype(jnp.float32)

    def attend(args):
        c, q_c = args
        start = c * CHUNK
        k_band = lax.dynamic_slice_in_dim(k_pad, start, BAND, axis=2)
        v_band = lax.dynamic_slice_in_dim(v_pad, start, BAND, axis=2)
        sc = jnp.einsum('bhqd,bhkd->bhqk', q_c, k_band).astype(jnp.float32) + bias
        key_pos = start - LEFT + jnp.arange(BAND, dtype=jnp.int32)
        sc = jnp.where((key_pos >= 0)[None, None, None, :], sc, MASK_VALUE)
        pr = jax.nn.softmax(sc, axis=-1).astype(v_band.dtype)
        return jnp.einsum('bhqk,bhkd->bhqd', pr, v_band)

    out = lax.map(attend, (jnp.arange(nc, dtype=jnp.int32), q_chunks))
    out = out.transpose(1, 0, 3, 2, 4).reshape(b, s, ATTN_WIDTH)
    return out


def setup_inputs(seed: int = 0) -> dict:
    key = jax.random.key(seed)
    ks = jax.random.split(key, 32)
    f32 = jnp.float32

    def w(k, shape, fan_in):
        return jax.random.normal(k, shape, f32) * (fan_in ** -0.5)

    def gain(k, shape):
        return 1.0 + 0.05 * jax.random.normal(k, shape, f32)

    return {
        "x": jax.random.normal(ks[0], (BATCH, SEQ, D_MODEL), f32),
        "p": jax.random.normal(ks[1], (DEPTH, BATCH, SEQ, PLE_DIM), f32),
        "ffn1_norm": gain(ks[2], (DEPTH, D_MODEL)),
        "ffn1_w_gate": w(ks[3], (DEPTH, D_MODEL, D_FF), D_MODEL),
        "ffn1_w_up": w(ks[4], (DEPTH, D_MODEL, D_FF), D_MODEL),
        "ffn1_w_down": w(ks[5], (DEPTH, D_FF, D_MODEL), D_FF),
        "mix_norm": gain(ks[6], (DEPTH, D_MODEL)),
        "w_in": w(ks[7], (DEPTH, D_MODEL, IN_COLS), D_MODEL),
        "pool_w": w(ks[8], (DEPTH, POOL_GROUPS, POOL_GROUP_DIM, POOL_GROUP_DIM), POOL_GROUP_DIM),
        "pool_scale": gain(ks[9], (DEPTH, POOL_WIDTH)),
        "q_norm": gain(ks[10], (DEPTH, HEAD_DIM)),
        "k_norm": gain(ks[11], (DEPTH, HEAD_DIM)),
        "rel_bias": 0.1 * jax.random.normal(ks[12], (DEPTH, N_HEADS, 2 * REL_CLIP + 1), f32),
        "w_br_pool": w(ks[13], (DEPTH, POOL_WIDTH, D_MODEL), POOL_WIDTH),
        "w_br_attn": w(ks[14], (DEPTH, ATTN_WIDTH, D_MODEL), ATTN_WIDTH),
        "w_branch_gate": w(ks[15], (DEPTH, D_MODEL, N_BRANCHES * D_MODEL), D_MODEL),
        "b_branch_gate": 0.02 * jax.random.normal(ks[16], (DEPTH, N_BRANCHES * D_MODEL), f32),
        "w_out": w(ks[17], (DEPTH, D_MODEL, D_MODEL), D_MODEL),
        "ffn2_norm": gain(ks[18], (DEPTH, D_MODEL)),
        "ffn2_w_gate": w(ks[19], (DEPTH, D_MODEL, D_FF), D_MODEL),
        "ffn2_w_up": w(ks[20], (DEPTH, D_MODEL, D_FF), D_MODEL),
        "ffn2_w_down": w(ks[21], (DEPTH, D_FF, D_MODEL), D_FF),
        "ple_norm": gain(ks[22], (DEPTH, D_MODEL)),
        "w_ple_gate": w(ks[23], (DEPTH, D_MODEL, D_MODEL), D_MODEL),
        "w_ple": w(ks[24], (DEPTH, PLE_DIM, D_MODEL), PLE_DIM),
    }


def reference(x, p, ffn1_norm, ffn1_w_gate, ffn1_w_up, ffn1_w_down, mix_norm, w_in,
              pool_w, pool_scale, q_norm, k_norm, rel_bias, w_br_pool, w_br_attn,
              w_branch_gate, b_branch_gate, w_out, ffn2_norm, ffn2_w_gate, ffn2_w_up,
              ffn2_w_down, ple_norm, w_ple_gate, w_ple):
    h = x
    for i in range(DEPTH):
        h = h + 0.5 * swiglu(rms_norm(h, ffn1_norm[i]), ffn1_w_gate[i], ffn1_w_up[i], ffn1_w_down[i])
        u = rms_norm(h, mix_norm[i])
        proj = u @ w_in[i]
        z_pool = proj[..., :POOL_WIDTH]
        q = proj[..., POOL_WIDTH:POOL_WIDTH + ATTN_WIDTH]
        k = proj[..., POOL_WIDTH + ATTN_WIDTH:POOL_WIDTH + 2 * ATTN_WIDTH]
        v = proj[..., POOL_WIDTH + 2 * ATTN_WIDTH:]
        y_pool = pool_mixer(z_pool, pool_w[i], pool_scale[i])
        y_attn = chunk_attention(q, k, v, q_norm[i], k_norm[i], rel_bias[i])
        gates = jax.nn.sigmoid(u @ w_branch_gate[i] + b_branch_gate[i])
        g_pool = gates[..., :D_MODEL]
        g_attn = gates[..., D_MODEL:]
        merged = g_pool * (y_pool @ w_br_pool[i]) + g_attn * (y_attn @ w_br_attn[i])
        h = h + merged @ w_out[i]
        h = h + 0.5 * swiglu(rms_norm(h, ffn2_norm[i]), ffn2_w_gate[i], ffn2_w_up[i], ffn2_w_down[i])
        ple_gate = jax.nn.sigmoid(rms_norm(h, ple_norm[i]) @ w_ple_gate[i])
        h = h + ple_gate * (p[i] @ w_ple[i])
    return h
```

```python
import functools

import jax
import jax.numpy as jnp
from jax import lax
from jax.experimental import pallas as pl
from jax.experimental.pallas import tpu as pltpu

F32 = jnp.float32
BF16 = jnp.bfloat16

EPS = 1e-6
MASK_VALUE = -1e30
CHUNK = 64
LEFT_CHUNKS = 8
LEFT = LEFT_CHUNKS * CHUNK
REL_CLIP = 256
N_HEADS = 8
HEAD_DIM = 128
POOL_WINDOWS = (2, 4, 8, 16)
POOL_GROUP_DIM = 256
POOL_HALO = 16

V7X_VMEM_BYTES = 64 * 1024 * 1024
VMEM_LIMIT_BYTES = V7X_VMEM_BYTES - 6 * 1024 * 1024

FFN_ROWS = 512
FFN_COLS = 512
PROJ_ROWS = 512
GATE_COLS = 1024
MIX_ROWS = 256
MIX_KEY_BLOCKS = 1 + LEFT // MIX_ROWS
MERGE_ROWS = 512
PLE_ROWS = 512


def _compiler_params(semantics):
    return pltpu.CompilerParams(dimension_semantics=semantics,
                                vmem_limit_bytes=VMEM_LIMIT_BYTES)


def _rms(xf, gain):
    return xf * lax.rsqrt(jnp.mean(xf * xf, axis=-1, keepdims=True) + EPS) * gain


def _dot(a, b):
    return jnp.dot(a, b, preferred_element_type=F32)


def _ffn_kernel(x_ref, gain_ref, next_gain_ref, wg_ref, wu_ref, wd_ref,
                o_ref, on_ref, xn_ref):
    j = pl.program_id(1)

    @pl.when(j == 0)
    def _():
        xf = x_ref[...]
        xn_ref[...] = _rms(xf, gain_ref[...]).astype(BF16)
        o_ref[...] = xf

    xn = xn_ref[...]
    g = _dot(xn, wg_ref[...])
    u = _dot(xn, wu_ref[...])
    h = (g * jax.nn.sigmoid(g) * u * 0.5).astype(BF16)
    o_ref[...] += _dot(h, wd_ref[...])

    @pl.when(j == pl.num_programs(1) - 1)
    def _():
        on_ref[...] = _rms(o_ref[...], next_gain_ref[...]).astype(BF16)


def _ffn(x, gain, next_gain, wg, wu, wd):
    s, d = x.shape
    f = wg.shape[1]
    tm, tf = FFN_ROWS, FFN_COLS
    return pl.pallas_call(
        _ffn_kernel,
        name="ffn",
        grid=(s // tm, f // tf),
        in_specs=[
            pl.BlockSpec((tm, d), lambda i, j: (i, 0)),
            pl.BlockSpec((1, d), lambda i, j: (0, 0)),
            pl.BlockSpec((1, d), lambda i, j: (0, 0)),
            pl.BlockSpec((d, tf), lambda i, j: (0, j)),
            pl.BlockSpec((d, tf), lambda i, j: (0, j)),
            pl.BlockSpec((tf, d), lambda i, j: (j, 0)),
        ],
        out_specs=[
            pl.BlockSpec((tm, d), lambda i, j: (i, 0)),
            pl.BlockSpec((tm, d), lambda i, j: (i, 0)),
        ],
        out_shape=[
            jax.ShapeDtypeStruct((s, d), F32),
            jax.ShapeDtypeStruct((s, d), BF16),
        ],
        scratch_shapes=[pltpu.VMEM((tm, d), BF16)],
        compiler_params=_compiler_params(("parallel", "arbitrary")),
    )(x, gain, next_gain, wg, wu, wd)


def _head_norm_store(acc, gain, scale, out_ref):
    for h in range(N_HEADS):
        sl = slice(h * HEAD_DIM, (h + 1) * HEAD_DIM)
        out_ref[:, sl] = (_rms(acc[:, sl], gain) * scale).astype(out_ref.dtype)


def _proj_kernel(u_ref, w_ref, qg_ref, kg_ref, z_ref, q_ref, k_ref, v_ref):
    j = pl.program_id(1)
    acc = _dot(u_ref[...], w_ref[...])

    @pl.when(j == 0)
    def _():
        z_ref[...] = acc

    @pl.when(j == 1)
    def _():
        _head_norm_store(acc, qg_ref[...], HEAD_DIM ** -0.5, q_ref)

    @pl.when(j == 2)
    def _():
        _head_norm_store(acc, kg_ref[...], 1.0, k_ref)

    @pl.when(j == 3)
    def _():
        v_ref[...] = acc.astype(v_ref.dtype)


def _proj(u, w_in, q_gain, k_gain):
    s, d = u.shape
    width = N_HEADS * HEAD_DIM
    tm = PROJ_ROWS
    row_block = pl.BlockSpec((tm, width), lambda i, j: (i, 0))
    return pl.pallas_call(
        _proj_kernel,
        name="proj",
        grid=(s // tm, w_in.shape[1] // width),
        in_specs=[
            pl.BlockSpec((tm, d), lambda i, j: (i, 0)),
            pl.BlockSpec((d, width), lambda i, j: (0, j)),
            pl.BlockSpec((1, HEAD_DIM), lambda i, j: (0, 0)),
            pl.BlockSpec((1, HEAD_DIM), lambda i, j: (0, 0)),
        ],
        out_specs=[row_block, row_block, row_block, row_block],
        out_shape=[
            jax.ShapeDtypeStruct((s, width), F32),
            jax.ShapeDtypeStruct((s, width), BF16),
            jax.ShapeDtypeStruct((s, width), BF16),
            jax.ShapeDtypeStruct((s, width), BF16),
        ],
        compiler_params=_compiler_params(("parallel", "arbitrary")),
    )(u, w_in, q_gain, k_gain)


def _gates_kernel(u_ref, w_ref, b_ref, o_ref):
    o_ref[...] = jax.nn.sigmoid(_dot(u_ref[...], w_ref[...]) + b_ref[...]).astype(o_ref.dtype)


def _gates(u, w, b):
    s, d = u.shape
    n = w.shape[1]
    tm, tn = PROJ_ROWS, GATE_COLS
    return pl.pallas_call(
        _gates_kernel,
        name="gates",
        grid=(s // tm, n // tn),
        in_specs=[
            pl.BlockSpec((tm, d), lambda i, j: (i, 0)),
            pl.BlockSpec((d, tn), lambda i, j: (0, j)),
            pl.BlockSpec((1, tn), lambda i, j: (0, j)),
        ],
        out_specs=pl.BlockSpec((tm, tn), lambda i, j: (i, j)),
        out_shape=jax.ShapeDtypeStruct((s, n), BF16),
        compiler_params=_compiler_params(("parallel", "arbitrary")),
    )(u, w, b)


def _mix_kernel(zc_ref, zh_ref, q_ref, *refs):
    nb = MIX_KEY_BLOCKS
    k_refs = refs[:nb]
    v_refs = refs[nb:2 * nb]
    bias_ref, pw_ref, ps_ref, yp_ref, ya_ref, zs_ref = refs[2 * nb:]
    i = pl.program_id(0)
    tq = MIX_ROWS

    zs_ref[:POOL_HALO, :] = jnp.where(i > 0, zh_ref[...], 0.0)
    zs_ref[POOL_HALO:, :] = zc_ref[...]
    t = i * tq + lax.broadcasted_iota(jnp.int32, (tq, 1), 0)
    for g, w in enumerate(POOL_WINDOWS):
        cols = slice(g * POOL_GROUP_DIM, (g + 1) * POOL_GROUP_DIM)
        acc = zs_ref[POOL_HALO:, cols]
        for back in range(1, w):
            acc = acc + zs_ref[POOL_HALO - back:POOL_HALO - back + tq, cols]
        count = jnp.minimum(t + 1, w).astype(F32)
        diff = (acc / count - zc_ref[:, cols]).astype(BF16)
        y = _dot(diff, pw_ref[g]) * ps_ref[:, cols]
        yp_ref[:, cols] = y.astype(yp_ref.dtype)

    penalties = [jnp.where(i + b >= nb - 1, 0.0, MASK_VALUE) for b in range(nb)]
    for h in range(N_HEADS):
        hs = slice(h * HEAD_DIM, (h + 1) * HEAD_DIM)
        qh = q_ref[:, hs]
        scores = []
        for b in range(nb):
            sc = lax.dot_general(qh, k_refs[b][:, hs], (((1,), (1,)), ((), ())),
                                 preferred_element_type=F32)
            scores.append(sc + bias_ref[h, :, b * tq:(b + 1) * tq] + penalties[b])
        m = scores[0].max(axis=-1, keepdims=True)
        for b in range(1, nb):
            m = jnp.maximum(m, scores[b].max(axis=-1, keepdims=True))
        denom = jnp.zeros((tq, 1), F32)
        out = jnp.zeros((tq, HEAD_DIM), F32)
        for b in range(nb):
            pr = jnp.exp(scores[b] - m)
            denom = denom + pr.sum(axis=-1, keepdims=True)
            out = out + _dot(pr.astype(BF16), v_refs[b][:, hs])
        ya_ref[:, hs] = (out / denom).astype(ya_ref.dtype)


def _attention_bias(rel_bias):
    tq, tk = MIX_ROWS, MIX_KEY_BLOCKS * MIX_ROWS
    r = jnp.arange(tq, dtype=jnp.int32)[:, None]
    c = jnp.arange(tk, dtype=jnp.int32)[None, :]
    rel_idx = jnp.clip(r - c + LEFT, -REL_CLIP, REL_CLIP) + REL_CLIP
    q_chunk = r // CHUNK
    k_chunk = c // CHUNK
    visible = (k_chunk >= q_chunk) & (k_chunk <= q_chunk + LEFT_CHUNKS)
    return jnp.where(visible[None], rel_bias[:, rel_idx], MASK_VALUE).astype(F32)


def _mix(z, q, k, v, bias, pool_w, pool_scale):
    s, width = z.shape
    tq, nb = MIX_ROWS, MIX_KEY_BLOCKS
    halo_per_tile = tq // POOL_HALO
    row_block = pl.BlockSpec((tq, width), lambda i: (i, 0))

    def key_block(b):
        return pl.BlockSpec((tq, width), lambda i: (jnp.maximum(i - (nb - 1 - b), 0), 0))

    return pl.pallas_call(
        _mix_kernel,
        name="mix",
        grid=(s // tq,),
        in_specs=[
            row_block,
            pl.BlockSpec((POOL_HALO, width),
                         lambda i: (jnp.maximum(i * halo_per_tile - 1, 0), 0)),
            row_block,
            *[key_block(b) for b in range(nb)],
            *[key_block(b) for b in range(nb)],
            pl.BlockSpec(bias.shape, lambda i: (0, 0, 0)),
            pl.BlockSpec(pool_w.shape, lambda i: (0, 0, 0)),
            pl.BlockSpec((1, width), lambda i: (0, 0)),
        ],
        out_specs=[row_block, row_block],
        out_shape=[
            jax.ShapeDtypeStruct((s, width), BF16),
            jax.ShapeDtypeStruct((s, width), BF16),
        ],
        scratch_shapes=[pltpu.VMEM((POOL_HALO + tq, width), F32)],
        compiler_params=_compiler_params(("arbitrary",)),
    )(z, z, q, *([k] * nb), *([v] * nb), bias, pool_w, pool_scale)


def _merge_kernel(h_ref, yp_ref, ya_ref, gp_ref, ga_ref, wa_ref, wb_ref, wo_ref, o_ref):
    merged = (gp_ref[...].astype(F32) * _dot(yp_ref[...], wa_ref[...])
              + ga_ref[...].astype(F32) * _dot(ya_ref[...], wb_ref[...]))
    o_ref[...] = h_ref[...] + _dot(merged.astype(BF16), wo_ref[...])


def _merge(h, y_pool, y_attn, gates, w_a, w_b, w_out):
    s, d = h.shape
    width = y_pool.shape[1]
    tm = MERGE_ROWS
    resident = functools.partial(pl.BlockSpec, pipeline_mode=pl.Buffered(1))
    return pl.pallas_call(
        _merge_kernel,
        name="merge",
        grid=(s // tm,),
        in_specs=[
            pl.BlockSpec((tm, d), lambda i: (i, 0)),
            pl.BlockSpec((tm, width), lambda i: (i, 0)),
            pl.BlockSpec((tm, width), lambda i: (i, 0)),
            pl.BlockSpec((tm, d), lambda i: (i, 0)),
            pl.BlockSpec((tm, d), lambda i: (i, 1)),
            resident(w_a.shape, lambda i: (0, 0)),
            resident(w_b.shape, lambda i: (0, 0)),
            resident(w_out.shape, lambda i: (0, 0)),
        ],
        out_specs=pl.BlockSpec((tm, d), lambda i: (i, 0)),
        out_shape=jax.ShapeDtypeStruct((s, d), F32),
        compiler_params=_compiler_params(("parallel",)),
    )(h, y_pool, y_attn, gates, gates, w_a, w_b, w_out)


def _ple_kernel(h_ref, t_ref, p_ref, wpg_ref, wple_ref, o_ref):
    gate = jax.nn.sigmoid(_dot(t_ref[...], wpg_ref[...]))
    emb = _dot(p_ref[...].astype(BF16), wple_ref[...])
    o_ref[...] = h_ref[...] + gate * emb


def _ple(h, t, p, w_pg, w_ple):
    s, d = h.shape
    tm = PLE_ROWS
    resident = functools.partial(pl.BlockSpec, pipeline_mode=pl.Buffered(1))
    return pl.pallas_call(
        _ple_kernel,
        name="ple",
        grid=(s // tm,),
        in_specs=[
            pl.BlockSpec((tm, d), lambda i: (i, 0)),
            pl.BlockSpec((tm, d), lambda i: (i, 0)),
            pl.BlockSpec((tm, p.shape[1]), lambda i: (i, 0)),
            resident(w_pg.shape, lambda i: (0, 0)),
            resident(w_ple.shape, lambda i: (0, 0)),
        ],
        out_specs=pl.BlockSpec((tm, d), lambda i: (i, 0)),
        out_shape=jax.ShapeDtypeStruct((s, d), F32),
        compiler_params=_compiler_params(("parallel",)),
    )(h, t, p, w_pg, w_ple)


def kernel(x, p, ffn1_norm, ffn1_w_gate, ffn1_w_up, ffn1_w_down, mix_norm, w_in, pool_w, pool_scale, q_norm, k_norm, rel_bias, w_br_pool, w_br_attn, w_branch_gate, b_branch_gate, w_out, ffn2_norm, ffn2_w_gate, ffn2_w_up, ffn2_w_down, ple_norm, w_ple_gate, w_ple):
    batch, seq, d_model = x.shape
    assert batch == 1, "the row tiling treats the sequence as the only row axis"
    depth = p.shape[0]
    h = x.reshape(seq, d_model)
    for i in range(depth):
        row = lambda a: a[i].reshape(1, -1)
        mm = lambda a: a[i].astype(BF16)
        h, u = _ffn(h, row(ffn1_norm), row(mix_norm),
                    mm(ffn1_w_gate), mm(ffn1_w_up), mm(ffn1_w_down))
        z, q, k, v = _proj(u, mm(w_in), row(q_norm), row(k_norm))
        gates = _gates(u, mm(w_branch_gate), row(b_branch_gate))
        y_pool, y_attn = _mix(z, q, k, v, _attention_bias(rel_bias[i]),
                              mm(pool_w), row(pool_scale))
        h = _merge(h, y_pool, y_attn, gates, mm(w_br_pool), mm(w_br_attn), mm(w_out))
        h, t = _ffn(h, row(ffn2_norm), row(ple_norm),
                    mm(ffn2_w_gate), mm(ffn2_w_up), mm(ffn2_w_down))
        h = _ple(h, t, p[i].reshape(seq, -1), mm(w_ple_gate), mm(w_ple))
    return h.reshape(batch, seq, d_model)
```

```python
import functools

import jax
import jax.numpy as jnp
from jax import lax
from jax.experimental import pallas as pl
from jax.experimental.pallas import tpu as pltpu

F32 = jnp.float32
BF16 = jnp.bfloat16

EPS = 1e-6
MASK_VALUE = -1e30
CHUNK = 64
LEFT_CHUNKS = 8
LEFT = LEFT_CHUNKS * CHUNK
REL_CLIP = 256
N_HEADS = 8
HEAD_DIM = 128
POOL_WINDOWS = (2, 4, 8, 16)
POOL_GROUP_DIM = 256
POOL_HALO = 16

V7X_VMEM_BYTES = 64 * 1024 * 1024
VMEM_LIMIT_BYTES = V7X_VMEM_BYTES - 6 * 1024 * 1024

FFN_ROWS = 1024
FFN_COLS = 512
UPROJ_ROWS = 512
MIX_ROWS = 256
MIX_KEY_BLOCKS = 1 + LEFT // MIX_ROWS
MIX_KEYS = MIX_KEY_BLOCKS * MIX_ROWS
BIAS_TABLE = 1024
MERGE_ROWS = 256
PLE_ROWS = 512


def _compiler_params(semantics):
    return pltpu.CompilerParams(dimension_semantics=semantics,
                                vmem_limit_bytes=VMEM_LIMIT_BYTES)


def _resident(shape):
    return pl.BlockSpec(shape, lambda *_: (0,) * len(shape), pipeline_mode=pl.Buffered(1))


def _rms(xf, gain):
    return xf * lax.rsqrt(jnp.mean(xf * xf, axis=-1, keepdims=True) + EPS) * gain


def _dot(a, b):
    return jnp.dot(a, b, preferred_element_type=F32)


def _ffn_kernel(x_ref, gain_ref, wg_ref, wu_ref, wd_ref, o_ref, xn_ref):
    @pl.when(pl.program_id(1) == 0)
    def _():
        xf = x_ref[...]
        xn_ref[...] = _rms(xf, gain_ref[...]).astype(BF16)
        o_ref[...] = xf

    xn = xn_ref[...]
    g = _dot(xn, wg_ref[...])
    u = _dot(xn, wu_ref[...])
    h = (g * jax.nn.sigmoid(g) * u * 0.5).astype(BF16)
    o_ref[...] += _dot(h, wd_ref[...])


def _ffn(x, gain, wg, wu, wd):
    s, d = x.shape
    f = wg.shape[1]
    tm, tf = FFN_ROWS, FFN_COLS
    return pl.pallas_call(
        _ffn_kernel,
        name="ffn",
        grid=(s // tm, f // tf),
        in_specs=[
            pl.BlockSpec((tm, d), lambda i, j: (i, 0)),
            pl.BlockSpec((1, d), lambda i, j: (0, 0)),
            pl.BlockSpec((d, tf), lambda i, j: (0, j)),
            pl.BlockSpec((d, tf), lambda i, j: (0, j)),
            pl.BlockSpec((tf, d), lambda i, j: (j, 0)),
        ],
        out_specs=pl.BlockSpec((tm, d), lambda i, j: (i, 0)),
        out_shape=jax.ShapeDtypeStruct((s, d), F32),
        scratch_shapes=[pltpu.VMEM((tm, d), BF16)],
        compiler_params=_compiler_params(("parallel", "arbitrary")),
    )(x, gain, wg, wu, wd)


def _head_norm_store(acc, gain, scale, out_ref):
    for h in range(N_HEADS):
        sl = slice(h * HEAD_DIM, (h + 1) * HEAD_DIM)
        out_ref[:, sl] = (_rms(acc[:, sl], gain) * scale).astype(out_ref.dtype)


def _uproj_kernel(h_ref, gain_ref, w_ref, qg_ref, kg_ref, z_ref, q_ref, k_ref, v_ref, u_ref):
    width = N_HEADS * HEAD_DIM
    u_ref[...] = _rms(h_ref[...], gain_ref[...]).astype(BF16)
    z_ref[...] = _dot(u_ref[...], w_ref[:, 0:width])
    _head_norm_store(_dot(u_ref[...], w_ref[:, width:2 * width]),
                     qg_ref[...], HEAD_DIM ** -0.5, q_ref)
    _head_norm_store(_dot(u_ref[...], w_ref[:, 2 * width:3 * width]),
                     kg_ref[...], 1.0, k_ref)
    v_ref[...] = _dot(u_ref[...], w_ref[:, 3 * width:4 * width]).astype(v_ref.dtype)


def _uproj(h, gain, w_in, q_gain, k_gain):
    s, d = h.shape
    width = N_HEADS * HEAD_DIM
    tm = UPROJ_ROWS
    row_block = pl.BlockSpec((tm, width), lambda i: (i, 0))
    return pl.pallas_call(
        _uproj_kernel,
        name="uproj",
        grid=(s // tm,),
        in_specs=[
            pl.BlockSpec((tm, d), lambda i: (i, 0)),
            _resident((1, d)),
            _resident(w_in.shape),
            _resident((1, HEAD_DIM)),
            _resident((1, HEAD_DIM)),
        ],
        out_specs=[row_block, row_block, row_block, row_block],
        out_shape=[
            jax.ShapeDtypeStruct((s, width), F32),
            jax.ShapeDtypeStruct((s, width), BF16),
            jax.ShapeDtypeStruct((s, width), BF16),
            jax.ShapeDtypeStruct((s, width), BF16),
        ],
        scratch_shapes=[pltpu.VMEM((tm, d), BF16)],
        compiler_params=_compiler_params(("parallel",)),
    )(h, gain, w_in, q_gain, k_gain)


def _build_attention_bias(i, table_ref, bias_ref):
    tq, tk = MIX_ROWS, MIX_KEYS
    r = lax.broadcasted_iota(jnp.int32, (tq, tk), 0)
    c = lax.broadcasted_iota(jnp.int32, (tq, tk), 1)
    q_chunk = r // CHUNK
    k_chunk = c // CHUNK
    first_key = (MIX_KEY_BLOCKS - 1 - i) * tq
    visible = (k_chunk >= q_chunk) & (k_chunk <= q_chunk + LEFT_CHUNKS) & (c >= first_key)
    for h in range(N_HEADS):
        rows = jnp.broadcast_to(table_ref[h:h + 1, :], (tq, BIAS_TABLE))
        rolled = pltpu.roll(rows, 0, 1, stride=1, stride_axis=0)
        bias_ref[h] = jnp.where(visible, rolled[:, :tk], MASK_VALUE)


def _mix_kernel(zc_ref, zh_ref, q_ref, *refs):
    nb = MIX_KEY_BLOCKS
    k_refs = refs[:nb]
    v_refs = refs[nb:2 * nb]
    table_ref, pw_ref, ps_ref, yp_ref, ya_ref, zs_ref, bias_ref = refs[2 * nb:]
    i = pl.program_id(0)
    tq = MIX_ROWS

    @pl.when(i < nb)
    def _():
        _build_attention_bias(i, table_ref, bias_ref)

    zs_ref[:POOL_HALO, :] = jnp.where(i > 0, zh_ref[...], 0.0)
    zs_ref[POOL_HALO:, :] = zc_ref[...]
    t = i * tq + lax.broadcasted_iota(jnp.int32, (tq, 1), 0)
    for g, w in enumerate(POOL_WINDOWS):
        cols = slice(g * POOL_GROUP_DIM, (g + 1) * POOL_GROUP_DIM)
        acc = zs_ref[POOL_HALO:, cols]
        for back in range(1, w):
            acc = acc + zs_ref[POOL_HALO - back:POOL_HALO - back + tq, cols]
        count = jnp.minimum(t + 1, w).astype(F32)
        diff = (acc / count - zc_ref[:, cols]).astype(BF16)
        y = _dot(diff, pw_ref[g]) * ps_ref[:, cols]
        yp_ref[:, cols] = y.astype(yp_ref.dtype)

    for h in range(N_HEADS):
        hs = slice(h * HEAD_DIM, (h + 1) * HEAD_DIM)
        qh = q_ref[:, hs]
        scores = []
        for b in range(nb):
            sc = lax.dot_general(qh, k_refs[b][:, hs], (((1,), (1,)), ((), ())),
                                 preferred_element_type=F32)
            scores.append(sc + bias_ref[h, :, b * tq:(b + 1) * tq])
        m = scores[0].max(axis=-1, keepdims=True)
        for b in range(1, nb):
            m = jnp.maximum(m, scores[b].max(axis=-1, keepdims=True))
        denom = jnp.zeros((tq, 1), F32)
        out = jnp.zeros((tq, HEAD_DIM), F32)
        for b in range(nb):
            pr = jnp.exp(scores[b] - m)
            denom = denom + pr.sum(axis=-1, keepdims=True)
            out = out + _dot(pr.astype(BF16), v_refs[b][:, hs])
        ya_ref[:, hs] = (out / denom).astype(ya_ref.dtype)


def _bias_table(rel_bias):
    far = rel_bias[:, 2 * REL_CLIP:]
    head = jnp.broadcast_to(far, (rel_bias.shape[0], LEFT - REL_CLIP))
    body = rel_bias[:, :0:-1]
    tail = jnp.broadcast_to(far, (rel_bias.shape[0],
                                  BIAS_TABLE - (LEFT - REL_CLIP) - 2 * REL_CLIP))
    return jnp.concatenate([head, body, tail], axis=1).astype(F32)


def _mix(z, q, k, v, table, pool_w, pool_scale):
    s, width = z.shape
    tq, nb = MIX_ROWS, MIX_KEY_BLOCKS
    halo_per_tile = tq // POOL_HALO
    row_block = pl.BlockSpec((tq, width), lambda i: (i, 0))

    def key_block(b):
        return pl.BlockSpec((tq, width), lambda i: (jnp.maximum(i - (nb - 1 - b), 0), 0))

    return pl.pallas_call(
        _mix_kernel,
        name="mix",
        grid=(s // tq,),
        in_specs=[
            row_block,
            pl.BlockSpec((POOL_HALO, width),
                         lambda i: (jnp.maximum(i * halo_per_tile - 1, 0), 0)),
            row_block,
            *[key_block(b) for b in range(nb)],
            *[key_block(b) for b in range(nb)],
            _resident(table.shape),
            _resident(pool_w.shape),
            _resident((1, width)),
        ],
        out_specs=[row_block, row_block],
        out_shape=[
            jax.ShapeDtypeStruct((s, width), BF16),
            jax.ShapeDtypeStruct((s, width), BF16),
        ],
        scratch_shapes=[
            pltpu.VMEM((POOL_HALO + tq, width), F32),
            pltpu.VMEM((N_HEADS, tq, MIX_KEYS), F32),
        ],
        compiler_params=_compiler_params(("arbitrary",)),
    )(z, z, q, *([k] * nb), *([v] * nb), table, pool_w, pool_scale)


def _merge_kernel(h_ref, yp_ref, ya_ref, gain_ref, wg_ref, bg_ref, wa_ref, wb_ref, wo_ref,
                  o_ref, u_ref):
    d = h_ref.shape[1]
    u_ref[...] = _rms(h_ref[...], gain_ref[...]).astype(BF16)
    g_pool = jax.nn.sigmoid(_dot(u_ref[...], wg_ref[:, :d]) + bg_ref[:, :d])
    merged = g_pool * _dot(yp_ref[...], wa_ref[...])
    g_attn = jax.nn.sigmoid(_dot(u_ref[...], wg_ref[:, d:]) + bg_ref[:, d:])
    merged = merged + g_attn * _dot(ya_ref[...], wb_ref[...])
    o_ref[...] = h_ref[...] + _dot(merged.astype(BF16), wo_ref[...])


def _merge(h, y_pool, y_attn, gain, w_g, b_g, w_a, w_b, w_out):
    s, d = h.shape
    width = y_pool.shape[1]
    tm = MERGE_ROWS
    return pl.pallas_call(
        _merge_kernel,
        name="merge",
        grid=(s // tm,),
        in_specs=[
            pl.BlockSpec((tm, d), lambda i: (i, 0)),
            pl.BlockSpec((tm, width), lambda i: (i, 0)),
            pl.BlockSpec((tm, width), lambda i: (i, 0)),
            _resident((1, d)),
            _resident(w_g.shape),
            _resident(b_g.shape),
            _resident(w_a.shape),
            _resident(w_b.shape),
            _resident(w_out.shape),
        ],
        out_specs=pl.BlockSpec((tm, d), lambda i: (i, 0)),
        out_shape=jax.ShapeDtypeStruct((s, d), F32),
        scratch_shapes=[pltpu.VMEM((tm, d), BF16)],
        compiler_params=_compiler_params(("parallel",)),
    )(h, y_pool, y_attn, gain, w_g, b_g, w_a, w_b, w_out)


def _ple_kernel(h_ref, p_ref, gain_ref, wpg_ref, wple_ref, o_ref):
    t = _rms(h_ref[...], gain_ref[...]).astype(BF16)
    gate = jax.nn.sigmoid(_dot(t, wpg_ref[...]))
    emb = _dot(p_ref[...].astype(BF16), wple_ref[...])
    o_ref[...] = h_ref[...] + gate * emb


def _ple(h, p, gain, w_pg, w_ple):
    s, d = h.shape
    tm = PLE_ROWS
    return pl.pallas_call(
        _ple_kernel,
        name="ple",
        grid=(s // tm,),
        in_specs=[
            pl.BlockSpec((tm, d), lambda i: (i, 0)),
            pl.BlockSpec((tm, p.shape[1]), lambda i: (i, 0)),
            _resident((1, d)),
            _resident(w_pg.shape),
            _resident(w_ple.shape),
        ],
        out_specs=pl.BlockSpec((tm, d), lambda i: (i, 0)),
        out_shape=jax.ShapeDtypeStruct((s, d), F32),
        compiler_params=_compiler_params(("parallel",)),
    )(h, p, gain, w_pg, w_ple)


def kernel(x, p, ffn1_norm, ffn1_w_gate, ffn1_w_up, ffn1_w_down, mix_norm, w_in, pool_w, pool_scale, q_norm, k_norm, rel_bias, w_br_pool, w_br_attn, w_branch_gate, b_branch_gate, w_out, ffn2_norm, ffn2_w_gate, ffn2_w_up, ffn2_w_down, ple_norm, w_ple_gate, w_ple):
    batch, seq, d_model = x.shape
    assert batch == 1, "the row tiling treats the sequence as the only row axis"
    depth = p.shape[0]
    h = x.reshape(seq, d_model)
    for i in range(depth):
        row = lambda a: a[i].reshape(1, -1)
        mm = lambda a: a[i].astype(BF16)
        h = _ffn(h, row(ffn1_norm), mm(ffn1_w_gate), mm(ffn1_w_up), mm(ffn1_w_down))
        z, q, k, v = _uproj(h, row(mix_norm), mm(w_in), row(q_norm), row(k_norm))
        y_pool, y_attn = _mix(z, q, k, v, _bias_table(rel_bias[i]),
                              mm(pool_w), row(pool_scale))
        h = _merge(h, y_pool, y_attn, row(mix_norm), mm(w_branch_gate), row(b_branch_gate),
                   mm(w_br_pool), mm(w_br_attn), mm(w_out))
        h = _ffn(h, row(ffn2_norm), mm(ffn2_w_gate), mm(ffn2_w_up), mm(ffn2_w_down))
        h = _ple(h, p[i].reshape(seq, -1), row(ple_norm), mm(w_ple_gate), mm(w_ple))
    return h.reshape(batch, seq, d_model)
```

```python
import math

import jax
import jax.numpy as jnp
from jax import lax
from jax.experimental import pallas as pl
from jax.experimental.pallas import tpu as pltpu

F32 = jnp.float32
BF16 = jnp.bfloat16

EPS = 1e-6
MASK_VALUE = -1e30
LOG2E = math.log2(math.e)
CHUNK = 64
LEFT_CHUNKS = 8
LEFT = LEFT_CHUNKS * CHUNK
REL_CLIP = 256
N_HEADS = 8
HEAD_DIM = 128
ATTN_WIDTH = N_HEADS * HEAD_DIM
POOL_WINDOWS = (2, 4, 8, 16)
POOL_GROUP_DIM = 256
POOL_HALO = 16

V7X_VMEM_BYTES = 64 * 1024 * 1024
VMEM_LIMIT_BYTES = V7X_VMEM_BYTES - 6 * 1024 * 1024

FFN_ROWS = 1024
FFN_COLS = 512
MIX_ROWS = 256
MIX_KEY_BLOCKS = 1 + LEFT // MIX_ROWS
MIX_KEYS = MIX_KEY_BLOCKS * MIX_ROWS
MIX_SUB_ROWS = 2 * CHUNK
MIX_SUB_KEYS = MIX_SUB_ROWS + LEFT
BIAS_TABLE = 1024
MERGE_ROWS = 256
PLE_ROWS = 512


def _compiler_params(semantics):
    return pltpu.CompilerParams(dimension_semantics=semantics,
                                vmem_limit_bytes=VMEM_LIMIT_BYTES)


def _resident(shape):
    return pl.BlockSpec(shape, lambda *_: (0,) * len(shape), pipeline_mode=pl.Buffered(1))


def _rms(xf, gain):
    return xf * lax.rsqrt(jnp.mean(xf * xf, axis=-1, keepdims=True) + EPS) * gain


def _dot(a, b):
    return jnp.dot(a, b, preferred_element_type=F32)


def _ffn_kernel(x_ref, gain_ref, wg_ref, wu_ref, wd_ref, o_ref, xn_ref):
    @pl.when(pl.program_id(1) == 0)
    def _():
        xf = x_ref[...]
        xn_ref[...] = _rms(xf, gain_ref[...]).astype(BF16)
        o_ref[...] = xf

    xn = xn_ref[...]
    g = _dot(xn, wg_ref[...])
    u = _dot(xn, wu_ref[...])
    h = (g * jax.nn.sigmoid(g) * u * 0.5).astype(BF16)
    o_ref[...] += _dot(h, wd_ref[...])


def _ffn(x, gain, wg, wu, wd):
    s, d = x.shape
    f = wg.shape[1]
    tm, tf = FFN_ROWS, FFN_COLS
    return pl.pallas_call(
        _ffn_kernel,
        name="ffn",
        grid=(s // tm, f // tf),
        in_specs=[
            pl.BlockSpec((tm, d), lambda i, j: (i, 0)),
            pl.BlockSpec((1, d), lambda i, j: (0, 0)),
            pl.BlockSpec((d, tf), lambda i, j: (0, j)),
            pl.BlockSpec((d, tf), lambda i, j: (0, j)),
            pl.BlockSpec((tf, d), lambda i, j: (j, 0)),
        ],
        out_specs=pl.BlockSpec((tm, d), lambda i, j: (i, 0)),
        out_shape=jax.ShapeDtypeStruct((s, d), F32),
        scratch_shapes=[pltpu.VMEM((tm, d), BF16)],
        compiler_params=_compiler_params(("parallel", "arbitrary")),
    )(x, gain, wg, wu, wd)


def _head_norm_store(acc, gain, scale, out_ref):
    for h in range(N_HEADS):
        sl = slice(h * HEAD_DIM, (h + 1) * HEAD_DIM)
        out_ref[:, sl] = (_rms(acc[:, sl], gain) * scale).astype(out_ref.dtype)


def _build_attention_bias(i, table_ref, bias_ref):
    tq, tk = MIX_ROWS, MIX_KEYS
    r = lax.broadcasted_iota(jnp.int32, (tq, tk), 0)
    c = lax.broadcasted_iota(jnp.int32, (tq, tk), 1)
    q_chunk = r // CHUNK
    k_chunk = c // CHUNK
    first_key = (MIX_KEY_BLOCKS - 1 - i) * tq
    visible = (k_chunk >= q_chunk) & (k_chunk <= q_chunk + LEFT_CHUNKS) & (c >= first_key)
    for h in range(N_HEADS):
        rows = jnp.broadcast_to(table_ref[h:h + 1, :], (tq, BIAS_TABLE))
        rolled = pltpu.roll(rows, 0, 1, stride=1, stride_axis=0)
        bias_ref[h] = jnp.where(visible, rolled[:, :tk] * LOG2E, MASK_VALUE)


def _window_sums(z, w):
    span = 1
    while span < w:
        z = z + pltpu.roll(z, span, 0)
        span *= 2
    return z


def _mixer_kernel(h_ref, gain_ref, w_ref, qg_ref, kg_ref, table_ref, pw_ref, ps_ref,
                  yp_ref, ya_ref, u_ref, q_ref, k_ring, v_ring, zs_ref, bias_ref):
    nb = MIX_KEY_BLOCKS
    tq = MIX_ROWS
    width = ATTN_WIDTH
    i = pl.program_id(0)

    @pl.when(i < nb)
    def _():
        _build_attention_bias(i, table_ref, bias_ref)

    @pl.when(i == 0)
    def _():
        k_ring[...] = jnp.zeros_like(k_ring)
        v_ring[...] = jnp.zeros_like(v_ring)
        zs_ref[:POOL_HALO, :] = jnp.zeros((POOL_HALO, width), F32)

    @pl.when(i > 0)
    def _():
        zs_ref[:POOL_HALO, :] = zs_ref[tq:, :]

    slot = lax.rem(i, nb)
    u_ref[...] = _rms(h_ref[...], gain_ref[...]).astype(BF16)
    zs_ref[POOL_HALO:, :] = _dot(u_ref[...], w_ref[:, 0:width])
    _head_norm_store(_dot(u_ref[...], w_ref[:, width:2 * width]),
                     qg_ref[...], HEAD_DIM ** -0.5 * LOG2E, q_ref)
    _head_norm_store(_dot(u_ref[...], w_ref[:, 2 * width:3 * width]),
                     kg_ref[...], 1.0, k_ring.at[slot])
    v_ring[slot] = _dot(u_ref[...], w_ref[:, 3 * width:4 * width]).astype(BF16)

    t = i * tq + lax.broadcasted_iota(jnp.int32, (tq, 1), 0)
    for g, w in enumerate(POOL_WINDOWS):
        cols = slice(g * POOL_GROUP_DIM, (g + 1) * POOL_GROUP_DIM)
        z = zs_ref[:, cols]
        inv_count = 1.0 / jnp.minimum(t + 1, w).astype(F32)
        diff = (_window_sums(z, w)[POOL_HALO:] * inv_count - z[POOL_HALO:]).astype(BF16)
        y = _dot(diff, pw_ref[g]) * ps_ref[:, cols]
        yp_ref[:, cols] = y.astype(yp_ref.dtype)

    slots = [lax.rem(i + 1 + b, nb) for b in range(nb)]
    for h in range(N_HEADS):
        hs = slice(h * HEAD_DIM, (h + 1) * HEAD_DIM)
        qh = q_ref[:, hs]
        scores = []
        for b in range(nb):
            sc = lax.dot_general(qh, k_ring[slots[b], :, hs], (((1,), (1,)), ((), ())),
                                 preferred_element_type=F32)
            scores.append(sc + bias_ref[h, :, b * tq:(b + 1) * tq])
        m = scores[0].max(axis=-1, keepdims=True)
        for b in range(1, nb):
            m = jnp.maximum(m, scores[b].max(axis=-1, keepdims=True))
        denom = jnp.zeros((tq, 1), F32)
        out = jnp.zeros((tq, HEAD_DIM), F32)
        for b in range(nb):
            pr = jnp.exp2(scores[b] - m)
            denom = denom + pr.sum(axis=-1, keepdims=True)
            out = out + _dot(pr.astype(BF16), v_ring[slots[b], :, hs])
        ya_ref[:, hs] = (out / denom).astype(ya_ref.dtype)


def _bias_table(rel_bias):
    far = rel_bias[:, 2 * REL_CLIP:]
    head = jnp.broadcast_to(far, (rel_bias.shape[0], LEFT - REL_CLIP))
    body = rel_bias[:, :0:-1]
    tail = jnp.broadcast_to(far, (rel_bias.shape[0],
                                  BIAS_TABLE - (LEFT - REL_CLIP) - 2 * REL_CLIP))
    return jnp.concatenate([head, body, tail], axis=1).astype(F32)


def _mixer(h, gain, w_in, q_gain, k_gain, table, pool_w, pool_scale):
    s, d = h.shape
    width = ATTN_WIDTH
    tq, nb = MIX_ROWS, MIX_KEY_BLOCKS
    row_block = pl.BlockSpec((tq, width), lambda i: (i, 0))
    return pl.pallas_call(
        _mixer_kernel,
        name="mixer",
        grid=(s // tq,),
        in_specs=[
            pl.BlockSpec((tq, d), lambda i: (i, 0)),
            _resident((1, d)),
            _resident(w_in.shape),
            _resident((1, HEAD_DIM)),
            _resident((1, HEAD_DIM)),
            _resident(table.shape),
            _resident(pool_w.shape),
            _resident((1, width)),
        ],
        out_specs=[row_block, row_block],
        out_shape=[
            jax.ShapeDtypeStruct((s, width), BF16),
            jax.ShapeDtypeStruct((s, width), BF16),
        ],
        scratch_shapes=[
            pltpu.VMEM((tq, d), BF16),
            pltpu.VMEM((tq, width), BF16),
            pltpu.VMEM((nb, tq, width), BF16),
            pltpu.VMEM((nb, tq, width), BF16),
            pltpu.VMEM((POOL_HALO + tq, width), F32),
            pltpu.VMEM((N_HEADS, tq, MIX_KEYS), F32),
        ],
        compiler_params=_compiler_params(("arbitrary",)),
    )(h, gain, w_in, q_gain, k_gain, table, pool_w, pool_scale)


def _merge_kernel(h_ref, yp_ref, ya_ref, gain_ref, wg_ref, bg_ref, wa_ref, wb_ref, wo_ref,
                  o_ref, u_ref):
    d = h_ref.shape[1]
    u_ref[...] = _rms(h_ref[...], gain_ref[...]).astype(BF16)
    g_pool = jax.nn.sigmoid(_dot(u_ref[...], wg_ref[:, :d]) + bg_ref[:, :d])
    merged = g_pool * _dot(yp_ref[...], wa_ref[...])
    g_attn = jax.nn.sigmoid(_dot(u_ref[...], wg_ref[:, d:]) + bg_ref[:, d:])
    merged = merged + g_attn * _dot(ya_ref[...], wb_ref[...])
    o_ref[...] = h_ref[...] + _dot(merged.astype(BF16), wo_ref[...])


def _merge(h, y_pool, y_attn, gain, w_g, b_g, w_a, w_b, w_out):
    s, d = h.shape
    width = y_pool.shape[1]
    tm = MERGE_ROWS
    return pl.pallas_call(
        _merge_kernel,
        name="merge",
        grid=(s // tm,),
        in_specs=[
            pl.BlockSpec((tm, d), lambda i: (i, 0)),
            pl.BlockSpec((tm, width), lambda i: (i, 0)),
            pl.BlockSpec((tm, width), lambda i: (i, 0)),
            _resident((1, d)),
            _resident(w_g.shape),
            _resident(b_g.shape),
            _resident(w_a.shape),
            _resident(w_b.shape),
            _resident(w_out.shape),
        ],
        out_specs=pl.BlockSpec((tm, d), lambda i: (i, 0)),
        out_shape=jax.ShapeDtypeStruct((s, d), F32),
        scratch_shapes=[pltpu.VMEM((tm, d), BF16)],
        compiler_params=_compiler_params(("parallel",)),
    )(h, y_pool, y_attn, gain, w_g, b_g, w_a, w_b, w_out)


def _ple_kernel(h_ref, p_ref, gain_ref, wpg_ref, wple_ref, o_ref):
    t = _rms(h_ref[...], gain_ref[...]).astype(BF16)
    gate = jax.nn.sigmoid(_dot(t, wpg_ref[...]))
    emb = _dot(p_ref[...].astype(BF16), wple_ref[...])
    o_ref[...] = h_ref[...] + gate * emb


def _ple(h, p, gain, w_pg, w_ple):
    s, d = h.shape
    tm = PLE_ROWS
    return pl.pallas_call(
        _ple_kernel,
        name="ple",
        grid=(s // tm,),
        in_specs=[
            pl.BlockSpec((tm, d), lambda i: (i, 0)),
            pl.BlockSpec((tm, p.shape[1]), lambda i: (i, 0)),
            _resident((1, d)),
            _resident(w_pg.shape),
            _resident(w_ple.shape),
        ],
        out_specs=pl.BlockSpec((tm, d), lambda i: (i, 0)),
        out_shape=jax.ShapeDtypeStruct((s, d), F32),
        compiler_params=_compiler_params(("parallel",)),
    )(h, p, gain, w_pg, w_ple)


def kernel(x, p, ffn1_norm, ffn1_w_gate, ffn1_w_up, ffn1_w_down, mix_norm, w_in, pool_w, pool_scale, q_norm, k_norm, rel_bias, w_br_pool, w_br_attn, w_branch_gate, b_branch_gate, w_out, ffn2_norm, ffn2_w_gate, ffn2_w_up, ffn2_w_down, ple_norm, w_ple_gate, w_ple):
    batch, seq, d_model = x.shape
    assert batch == 1, "the row tiling treats the sequence as the only row axis"
    depth = p.shape[0]
    h = x.reshape(seq, d_model)
    for i in range(depth):
        row = lambda a: a[i].reshape(1, -1)
        mm = lambda a: a[i].astype(BF16)
        h = _ffn(h, row(ffn1_norm), mm(ffn1_w_gate), mm(ffn1_w_up), mm(ffn1_w_down))
        y_pool, y_attn = _mixer(h, row(mix_norm), mm(w_in), row(q_norm), row(k_norm),
                                _bias_table(rel_bias[i]), mm(pool_w), row(pool_scale))
        h = _merge(h, y_pool, y_attn, row(mix_norm), mm(w_branch_gate), row(b_branch_gate),
                   mm(w_br_pool), mm(w_br_attn), mm(w_out))
        h = _ffn(h, row(ffn2_norm), mm(ffn2_w_gate), mm(ffn2_w_up), mm(ffn2_w_down))
        h = _ple(h, p[i].reshape(seq, -1), row(ple_norm), mm(w_ple_gate), mm(w_ple))
    return h.reshape(batch, seq, d_model)
```

```python
import math

import jax
import jax.numpy as jnp
from jax import lax
from jax.experimental import pallas as pl
from jax.experimental.pallas import tpu as pltpu

F32 = jnp.float32
BF16 = jnp.bfloat16

EPS = 1e-6
MASK_VALUE = -1e30
LOG2E = math.log2(math.e)
CHUNK = 64
LEFT_CHUNKS = 8
LEFT = LEFT_CHUNKS * CHUNK
REL_CLIP = 256
N_HEADS = 8
HEAD_DIM = 128
ATTN_WIDTH = N_HEADS * HEAD_DIM
POOL_WINDOWS = (2, 4, 8, 16)
POOL_GROUP_DIM = 256
POOL_HALO = 16

V7X_VMEM_BYTES = 64 * 1024 * 1024
VMEM_LIMIT_BYTES = V7X_VMEM_BYTES - 6 * 1024 * 1024

FFN_ROWS = 1024
FFN_COLS = 512
MIX_ROWS = 256
MIX_KEY_BLOCKS = 1 + LEFT // MIX_ROWS
MIX_KEYS = MIX_KEY_BLOCKS * MIX_ROWS
BIAS_TABLE = 1024
MERGE_ROWS = 256
PLE_ROWS = 512


def _compiler_params(semantics):
    return pltpu.CompilerParams(dimension_semantics=semantics,
                                vmem_limit_bytes=VMEM_LIMIT_BYTES)


def _resident(shape):
    return pl.BlockSpec(shape, lambda *_: (0,) * len(shape), pipeline_mode=pl.Buffered(1))


def _rms(xf, gain):
    return xf * lax.rsqrt(jnp.mean(xf * xf, axis=-1, keepdims=True) + EPS) * gain


def _dot(a, b):
    return jnp.dot(a, b, preferred_element_type=F32)


def _ffn_kernel(x_ref, gain_ref, wg_ref, wu_ref, wd_ref, *refs):
    n_cast = (len(refs) - 2) // 2
    cast_in, o_ref, cast_out, xn_ref = (refs[:n_cast], refs[n_cast],
                                        refs[n_cast + 1:2 * n_cast + 1], refs[-1])

    @pl.when(pl.program_id(1) == 0)
    def _():
        xf = x_ref[...]
        xn_ref[...] = _rms(xf, gain_ref[...]).astype(BF16)
        o_ref[...] = xf

    xn = xn_ref[...]
    g = _dot(xn, wg_ref[...])
    u = _dot(xn, wu_ref[...])
    h = (g * jax.nn.sigmoid(g) * u * 0.5).astype(BF16)
    o_ref[...] += _dot(h, wd_ref[...])

    for src, dst in zip(cast_in, cast_out):
        dst[...] = src[...].astype(dst.dtype)


def _ffn(x, gain, wg, wu, wd, next_weights=None):
    s, d = x.shape
    f = wg.shape[1]
    tm, tf = FFN_ROWS, FFN_COLS
    ni, nj = s // tm, f // tf
    in_specs = [
        pl.BlockSpec((tm, d), lambda i, j: (i, 0)),
        pl.BlockSpec((1, d), lambda i, j: (0, 0)),
        pl.BlockSpec((d, tf), lambda i, j: (0, j)),
        pl.BlockSpec((d, tf), lambda i, j: (0, j)),
        pl.BlockSpec((tf, d), lambda i, j: (j, 0)),
    ]
    out_specs = [pl.BlockSpec((tm, d), lambda i, j: (i, 0))]
    out_shape = [jax.ShapeDtypeStruct((s, d), F32)]
    cast = []
    if next_weights is not None:
        cast = list(next_weights)
        up_spec = pl.BlockSpec((d // ni, f // nj), lambda i, j: (i, j))
        down_spec = pl.BlockSpec((f // nj, d // ni), lambda i, j: (j, i))
        in_specs += [up_spec, up_spec, down_spec]
        out_specs += [up_spec, up_spec, down_spec]
        out_shape += [jax.ShapeDtypeStruct(w.shape, BF16) for w in cast]
    outs = pl.pallas_call(
        _ffn_kernel,
        name="ffn",
        grid=(ni, nj),
        in_specs=in_specs,
        out_specs=out_specs,
        out_shape=out_shape,
        scratch_shapes=[pltpu.VMEM((tm, d), BF16)],
        compiler_params=_compiler_params(("parallel", "arbitrary")),
    )(x, gain, wg, wu, wd, *cast)
    return outs[0], tuple(outs[1:])


def _head_norm_store(acc, gain, scale, out_ref):
    for h in range(N_HEADS):
        sl = slice(h * HEAD_DIM, (h + 1) * HEAD_DIM)
        out_ref[:, sl] = (_rms(acc[:, sl], gain) * scale).astype(out_ref.dtype)


def _build_attention_bias(i, table_ref, bias_ref):
    tq, tk = MIX_ROWS, MIX_KEYS
    r = lax.broadcasted_iota(jnp.int32, (tq, tk), 0)
    c = lax.broadcasted_iota(jnp.int32, (tq, tk), 1)
    q_chunk = r // CHUNK
    k_chunk = c // CHUNK
    first_key = (MIX_KEY_BLOCKS - 1 - i) * tq
    visible = (k_chunk >= q_chunk) & (k_chunk <= q_chunk + LEFT_CHUNKS) & (c >= first_key)
    for h in range(N_HEADS):
        rows = jnp.broadcast_to(table_ref[h:h + 1, :], (tq, BIAS_TABLE))
        rolled = pltpu.roll(rows, 0, 1, stride=1, stride_axis=0)
        bias_ref[h] = jnp.where(visible, rolled[:, :tk] * LOG2E, MASK_VALUE)


def _window_sums(z, w):
    span = 1
    while span < w:
        z = z + pltpu.roll(z, span, 0)
        span *= 2
    return z


def _mixer_kernel(h_ref, gain_ref, w_ref, qg_ref, kg_ref, table_ref, pw_ref, ps_ref,
                  yp_ref, ya_ref, u_ref, q_ref, k_ring, v_ring, zs_ref, bias_ref):
    nb = MIX_KEY_BLOCKS
    tq = MIX_ROWS
    width = ATTN_WIDTH
    i = pl.program_id(0)

    @pl.when(i < nb)
    def _():
        _build_attention_bias(i, table_ref, bias_ref)

    @pl.when(i == 0)
    def _():
        k_ring[...] = jnp.zeros_like(k_ring)
        v_ring[...] = jnp.zeros_like(v_ring)
        zs_ref[:POOL_HALO, :] = jnp.zeros((POOL_HALO, width), F32)

    @pl.when(i > 0)
    def _():
        zs_ref[:POOL_HALO, :] = zs_ref[tq:, :]

    slot = lax.rem(i, nb)
    u_ref[...] = _rms(h_ref[...], gain_ref[...]).astype(BF16)
    zs_ref[POOL_HALO:, :] = _dot(u_ref[...], w_ref[:, 0:width])
    _head_norm_store(_dot(u_ref[...], w_ref[:, width:2 * width]),
                     qg_ref[...], HEAD_DIM ** -0.5 * LOG2E, q_ref)
    _head_norm_store(_dot(u_ref[...], w_ref[:, 2 * width:3 * width]),
                     kg_ref[...], 1.0, k_ring.at[slot])
    v_ring[slot] = _dot(u_ref[...], w_ref[:, 3 * width:4 * width]).astype(BF16)

    t = i * tq + lax.broadcasted_iota(jnp.int32, (tq, 1), 0)
    for g, w in enumerate(POOL_WINDOWS):
        cols = slice(g * POOL_GROUP_DIM, (g + 1) * POOL_GROUP_DIM)
        z = zs_ref[:, cols]
        inv_count = 1.0 / jnp.minimum(t + 1, w).astype(F32)
        diff = (_window_sums(z, w)[POOL_HALO:] * inv_count - z[POOL_HALO:]).astype(BF16)
        y = _dot(diff, pw_ref[g]) * ps_ref[:, cols]
        yp_ref[:, cols] = y.astype(yp_ref.dtype)

    slots = [lax.rem(i + 1 + b, nb) for b in range(nb)]
    for h in range(N_HEADS):
        hs = slice(h * HEAD_DIM, (h + 1) * HEAD_DIM)
        qh = q_ref[:, hs]
        scores = []
        for b in range(nb):
            sc = lax.dot_general(qh, k_ring[slots[b], :, hs], (((1,), (1,)), ((), ())),
                                 preferred_element_type=F32)
            scores.append(sc + bias_ref[h, :, b * tq:(b + 1) * tq])
        m = scores[0].max(axis=-1, keepdims=True)
        for b in range(1, nb):
            m = jnp.maximum(m, scores[b].max(axis=-1, keepdims=True))
        denom = jnp.zeros((tq, 1), F32)
        out = jnp.zeros((tq, HEAD_DIM), F32)
        for b in range(nb):
            pr = jnp.exp2(scores[b] - m)
            denom = denom + pr.sum(axis=-1, keepdims=True)
            out = out + _dot(pr.astype(BF16), v_ring[slots[b], :, hs])
        ya_ref[:, hs] = (out / denom).astype(ya_ref.dtype)


def _bias_table(rel_bias):
    far = rel_bias[:, 2 * REL_CLIP:]
    head = jnp.broadcast_to(far, (rel_bias.shape[0], LEFT - REL_CLIP))
    body = rel_bias[:, :0:-1]
    tail = jnp.broadcast_to(far, (rel_bias.shape[0],
                                  BIAS_TABLE - (LEFT - REL_CLIP) - 2 * REL_CLIP))
    return jnp.concatenate([head, body, tail], axis=1).astype(F32)


def _mixer(h, gain, w_in, q_gain, k_gain, table, pool_w, pool_scale):
    s, d = h.shape
    width = ATTN_WIDTH
    tq, nb = MIX_ROWS, MIX_KEY_BLOCKS
    row_block = pl.BlockSpec((tq, width), lambda i: (i, 0))
    return pl.pallas_call(
        _mixer_kernel,
        name="mixer",
        grid=(s // tq,),
        in_specs=[
            pl.BlockSpec((tq, d), lambda i: (i, 0)),
            _resident((1, d)),
            _resident(w_in.shape),
            _resident((1, HEAD_DIM)),
            _resident((1, HEAD_DIM)),
            _resident(table.shape),
            _resident(pool_w.shape),
            _resident((1, width)),
        ],
        out_specs=[row_block, row_block],
        out_shape=[
            jax.ShapeDtypeStruct((s, width), BF16),
            jax.ShapeDtypeStruct((s, width), BF16),
        ],
        scratch_shapes=[
            pltpu.VMEM((tq, d), BF16),
            pltpu.VMEM((tq, width), BF16),
            pltpu.VMEM((nb, tq, width), BF16),
            pltpu.VMEM((nb, tq, width), BF16),
            pltpu.VMEM((POOL_HALO + tq, width), F32),
            pltpu.VMEM((N_HEADS, tq, MIX_KEYS), F32),
        ],
        compiler_params=_compiler_params(("arbitrary",)),
    )(h, gain, w_in, q_gain, k_gain, table, pool_w, pool_scale)


def _merge_kernel(h_ref, yp_ref, ya_ref, gain_ref, wg_ref, bg_ref, wa_ref, wb_ref, wo_ref,
                  o_ref, u_ref):
    d = h_ref.shape[1]
    u_ref[...] = _rms(h_ref[...], gain_ref[...]).astype(BF16)
    g_pool = jax.nn.sigmoid(_dot(u_ref[...], wg_ref[:, :d]) + bg_ref[:, :d])
    merged = g_pool * _dot(yp_ref[...], wa_ref[...])
    g_attn = jax.nn.sigmoid(_dot(u_ref[...], wg_ref[:, d:]) + bg_ref[:, d:])
    merged = merged + g_attn * _dot(ya_ref[...], wb_ref[...])
    o_ref[...] = h_ref[...] + _dot(merged.astype(BF16), wo_ref[...])


def _merge(h, y_pool, y_attn, gain, w_g, b_g, w_a, w_b, w_out):
    s, d = h.shape
    width = y_pool.shape[1]
    tm = MERGE_ROWS
    return pl.pallas_call(
        _merge_kernel,
        name="merge",
        grid=(s // tm,),
        in_specs=[
            pl.BlockSpec((tm, d), lambda i: (i, 0)),
            pl.BlockSpec((tm, width), lambda i: (i, 0)),
            pl.BlockSpec((tm, width), lambda i: (i, 0)),
            _resident((1, d)),
            _resident(w_g.shape),
            _resident(b_g.shape),
            _resident(w_a.shape),
            _resident(w_b.shape),
            _resident(w_out.shape),
        ],
        out_specs=pl.BlockSpec((tm, d), lambda i: (i, 0)),
        out_shape=jax.ShapeDtypeStruct((s, d), F32),
        scratch_shapes=[pltpu.VMEM((tm, d), BF16)],
        compiler_params=_compiler_params(("parallel",)),
    )(h, y_pool, y_attn, gain, w_g, b_g, w_a, w_b, w_out)


def _ple_kernel(h_ref, p_ref, gain_ref, wpg_ref, wple_ref, o_ref):
    t = _rms(h_ref[...], gain_ref[...]).astype(BF16)
    gate = jax.nn.sigmoid(_dot(t, wpg_ref[...]))
    emb = _dot(p_ref[...].astype(BF16), wple_ref[...])
    o_ref[...] = h_ref[...] + gate * emb


def _ple(h, p, gain, w_pg, w_ple):
    s, d = h.shape
    tm = PLE_ROWS
    return pl.pallas_call(
        _ple_kernel,
        name="ple",
        grid=(s // tm,),
        in_specs=[
            pl.BlockSpec((tm, d), lambda i: (i, 0)),
            pl.BlockSpec((tm, p.shape[1]), lambda i: (i, 0)),
            _resident((1, d)),
            _resident(w_pg.shape),
            _resident(w_ple.shape),
        ],
        out_specs=pl.BlockSpec((tm, d), lambda i: (i, 0)),
        out_shape=jax.ShapeDtypeStruct((s, d), F32),
        compiler_params=_compiler_params(("parallel",)),
    )(h, p, gain, w_pg, w_ple)


def kernel(x, p, ffn1_norm, ffn1_w_gate, ffn1_w_up, ffn1_w_down, mix_norm, w_in, pool_w, pool_scale, q_norm, k_norm, rel_bias, w_br_pool, w_br_attn, w_branch_gate, b_branch_gate, w_out, ffn2_norm, ffn2_w_gate, ffn2_w_up, ffn2_w_down, ple_norm, w_ple_gate, w_ple):
    batch, seq, d_model = x.shape
    assert batch == 1, "the row tiling treats the sequence as the only row axis"
    depth = p.shape[0]
    h = x.reshape(seq, d_model)
    for i in range(depth):
        row = lambda a: a[i].reshape(1, -1)
        mm = lambda a: a[i].astype(BF16)
        h, ffn2_weights = _ffn(h, row(ffn1_norm), mm(ffn1_w_gate), mm(ffn1_w_up), mm(ffn1_w_down),
                               next_weights=(ffn2_w_gate[i], ffn2_w_up[i], ffn2_w_down[i]))
        y_pool, y_attn = _mixer(h, row(mix_norm), mm(w_in), row(q_norm), row(k_norm),
                                _bias_table(rel_bias[i]), mm(pool_w), row(pool_scale))
        h = _merge(h, y_pool, y_attn, row(mix_norm), mm(w_branch_gate), row(b_branch_gate),
                   mm(w_br_pool), mm(w_br_attn), mm(w_out))
        h, _ = _ffn(h, row(ffn2_norm), *ffn2_weights)
        h = _ple(h, p[i].reshape(seq, -1), row(ple_norm), mm(w_ple_gate), mm(w_ple))
    return h.reshape(batch, seq, d_model)
```

```python
import functools
import math

import jax
import jax.numpy as jnp
from jax import lax
from jax.experimental import pallas as pl
from jax.experimental.pallas import tpu as pltpu

F32 = jnp.float32
BF16 = jnp.bfloat16

EPS = 1e-6
MASK_VALUE = -1e30
LOG2E = math.log2(math.e)
CHUNK = 64
LEFT_CHUNKS = 8
LEFT = LEFT_CHUNKS * CHUNK
REL_CLIP = 256
N_HEADS = 8
HEAD_DIM = 128
ATTN_WIDTH = N_HEADS * HEAD_DIM
POOL_WINDOWS = (2, 4, 8, 16)
POOL_GROUP_DIM = 256
POOL_HALO = 16

V7X_VMEM_BYTES = 64 * 1024 * 1024
VMEM_LIMIT_BYTES = V7X_VMEM_BYTES - 6 * 1024 * 1024

FFN_ROWS = 1024
FFN_COLS = 512
FFN_HEAD_TILES = 1
FFN_HEAD_COLS = 256
BF16_SUBLANES = 16
MIX_ROWS = 256
MIX_KEY_BLOCKS = 1 + LEFT // MIX_ROWS
MIX_KEYS = MIX_KEY_BLOCKS * MIX_ROWS
BIAS_TABLE = 1024
MERGE_ROWS = 256
PLE_ROWS = 512


def _compiler_params(semantics):
    return pltpu.CompilerParams(dimension_semantics=semantics,
                                vmem_limit_bytes=VMEM_LIMIT_BYTES)


def _resident(shape):
    return pl.BlockSpec(shape, lambda *_: (0,) * len(shape), pipeline_mode=pl.Buffered(1))


def _rms(xf, gain):
    return xf * lax.rsqrt(jnp.mean(xf * xf, axis=-1, keepdims=True) + EPS) * gain


def _dot(a, b):
    return jnp.dot(a, b, preferred_element_type=F32)


def _ffn_kernel(x_ref, gain_ref, wg_ref, wu_ref, wd_ref, *refs, round_weights, aliased):
    refs = refs[1:] if aliased else refs
    o_ref, xn_ref = refs[0], refs[-1]
    if round_weights:
        for src, dst in zip((wg_ref, wu_ref, wd_ref), refs[1:4]):
            dst[...] = src[...].astype(dst.dtype)
        wg_ref, wu_ref, wd_ref = refs[1:4]

    @pl.when(pl.program_id(1) == 0)
    def _():
        xf = x_ref[...]
        xn_ref[...] = _rms(xf, gain_ref[...]).astype(BF16)
        o_ref[...] = xf

    xn = xn_ref[...]
    g = _dot(xn, wg_ref[...])
    u = _dot(xn, wu_ref[...])
    h = (g * jax.nn.sigmoid(g) * u * 0.5).astype(BF16)
    o_ref[...] += _dot(h, wd_ref[...])


def _ffn_specs(d, tm, tf, first_tile):
    row_block = pl.BlockSpec((tm, d), lambda i, j: (i + first_tile, 0))
    weight_specs = [
        pl.BlockSpec((d, tf), lambda i, j: (0, j)),
        pl.BlockSpec((d, tf), lambda i, j: (0, j)),
        pl.BlockSpec((tf, d), lambda i, j: (j, 0)),
    ]
    return row_block, [row_block, pl.BlockSpec((1, d), lambda i, j: (0, 0))] + weight_specs, weight_specs


def _ffn_head(x, gain, wg32, wu32, wd32):
    s, d = x.shape
    f = wg32.shape[1]
    tm, tf = FFN_ROWS, FFN_HEAD_COLS
    row_block, in_specs, weight_specs = _ffn_specs(d, tm, tf, 0)
    outs = pl.pallas_call(
        functools.partial(_ffn_kernel, round_weights=True, aliased=False),
        name="ffn_head",
        grid=(FFN_HEAD_TILES, f // tf),
        in_specs=in_specs,
        out_specs=[row_block] + weight_specs,
        out_shape=[jax.ShapeDtypeStruct((s, d), F32)]
        + [jax.ShapeDtypeStruct(w.shape, BF16) for w in (wg32, wu32, wd32)],
        scratch_shapes=[pltpu.VMEM((tm, d), BF16)],
        compiler_params=_compiler_params(("arbitrary", "arbitrary")),
    )(x, gain, wg32, wu32, wd32)
    return outs[0], tuple(outs[1:])


def _ffn(x, gain, wg, wu, wd, into=None):
    s, d = x.shape
    f = wg.shape[1]
    tm, tf = FFN_ROWS, FFN_COLS
    first_tile = 0 if into is None else FFN_HEAD_TILES
    row_block, in_specs, _ = _ffn_specs(d, tm, tf, first_tile)
    operands = [x, gain, wg, wu, wd]
    aliases = {}
    if into is not None:
        in_specs.append(pl.BlockSpec(memory_space=pl.ANY))
        aliases = {len(operands): 0}
        operands.append(into)
    return pl.pallas_call(
        functools.partial(_ffn_kernel, round_weights=False, aliased=into is not None),
        name="ffn",
        grid=(s // tm - first_tile, f // tf),
        in_specs=in_specs,
        out_specs=row_block,
        out_shape=jax.ShapeDtypeStruct((s, d), F32),
        input_output_aliases=aliases,
        scratch_shapes=[pltpu.VMEM((tm, d), BF16)],
        compiler_params=_compiler_params(("parallel", "arbitrary")),
    )(*operands)


def _head_norm_store(acc, gain, scale, out_ref):
    for h in range(N_HEADS):
        sl = slice(h * HEAD_DIM, (h + 1) * HEAD_DIM)
        out_ref[:, sl] = (_rms(acc[:, sl], gain) * scale).astype(out_ref.dtype)


def _build_attention_bias(i, table_ref, bias_ref):
    tq, tk = MIX_ROWS, MIX_KEYS
    r = lax.broadcasted_iota(jnp.int32, (tq, tk), 0)
    c = lax.broadcasted_iota(jnp.int32, (tq, tk), 1)
    q_chunk = r // CHUNK
    k_chunk = c // CHUNK
    first_key = (MIX_KEY_BLOCKS - 1 - i) * tq
    visible = (k_chunk >= q_chunk) & (k_chunk <= q_chunk + LEFT_CHUNKS) & (c >= first_key)
    for h in range(N_HEADS):
        rows = jnp.broadcast_to(table_ref[h:h + 1, :], (tq, BIAS_TABLE))
        rolled = pltpu.roll(rows, 0, 1, stride=1, stride_axis=0)
        bias_ref[h] = jnp.where(visible, rolled[:, :tk] * LOG2E, MASK_VALUE)


def _window_sums(z, w):
    span = 1
    while span < w:
        z = z + pltpu.roll(z, span, 0)
        span *= 2
    return z


def _mixer_kernel(h_ref, gain_ref, w_ref, qg_ref, kg_ref, table_ref, pw_ref, ps_ref, *refs):
    n_cast = (len(refs) - 8) // 2
    cast_in, (yp_ref, ya_ref), cast_out = refs[:n_cast], refs[n_cast:n_cast + 2], refs[n_cast + 2:-6]
    u_ref, q_ref, k_ring, v_ring, zs_ref, bias_ref = refs[-6:]
    nb = MIX_KEY_BLOCKS
    tq = MIX_ROWS
    width = ATTN_WIDTH
    i = pl.program_id(0)

    @pl.when(i < nb)
    def _():
        _build_attention_bias(i, table_ref, bias_ref)

    @pl.when(i == 0)
    def _():
        k_ring[...] = jnp.zeros_like(k_ring)
        v_ring[...] = jnp.zeros_like(v_ring)
        zs_ref[:POOL_HALO, :] = jnp.zeros((POOL_HALO, width), F32)

    @pl.when(i > 0)
    def _():
        zs_ref[:POOL_HALO, :] = zs_ref[tq:, :]

    def round_weights(part, parts=4):
        for src, dst in list(zip(cast_in, cast_out))[part::parts]:
            dst[...] = src[...].astype(dst.dtype)

    slot = lax.rem(i, nb)
    u_ref[...] = _rms(h_ref[...], gain_ref[...]).astype(BF16)
    zs_ref[POOL_HALO:, :] = _dot(u_ref[...], w_ref[:, 0:width])
    round_weights(0)
    _head_norm_store(_dot(u_ref[...], w_ref[:, width:2 * width]),
                     qg_ref[...], HEAD_DIM ** -0.5 * LOG2E, q_ref)
    round_weights(1)
    _head_norm_store(_dot(u_ref[...], w_ref[:, 2 * width:3 * width]),
                     kg_ref[...], 1.0, k_ring.at[slot])
    round_weights(2)
    v_ring[slot] = _dot(u_ref[...], w_ref[:, 3 * width:4 * width]).astype(BF16)
    round_weights(3)

    t = i * tq + lax.broadcasted_iota(jnp.int32, (tq, 1), 0)
    for g, w in enumerate(POOL_WINDOWS):
        cols = slice(g * POOL_GROUP_DIM, (g + 1) * POOL_GROUP_DIM)
        z = zs_ref[:, cols]
        inv_count = 1.0 / jnp.minimum(t + 1, w).astype(F32)
        diff = (_window_sums(z, w)[POOL_HALO:] * inv_count - z[POOL_HALO:]).astype(BF16)
        y = _dot(diff, pw_ref[g]) * ps_ref[:, cols]
        yp_ref[:, cols] = y.astype(yp_ref.dtype)

    slots = [lax.rem(i + 1 + b, nb) for b in range(nb)]
    for h in range(N_HEADS):
        hs = slice(h * HEAD_DIM, (h + 1) * HEAD_DIM)
        qh = q_ref[:, hs]
        scores = []
        for b in range(nb):
            sc = lax.dot_general(qh, k_ring[slots[b], :, hs], (((1,), (1,)), ((), ())),
                                 preferred_element_type=F32)
            scores.append(sc + bias_ref[h, :, b * tq:(b + 1) * tq])
        m = scores[0].max(axis=-1, keepdims=True)
        for b in range(1, nb):
            m = jnp.maximum(m, scores[b].max(axis=-1, keepdims=True))
        denom = jnp.zeros((tq, 1), F32)
        out = jnp.zeros((tq, HEAD_DIM), F32)
        for b in range(nb):
            pr = jnp.exp2(scores[b] - m)
            denom = denom + pr.sum(axis=-1, keepdims=True)
            out = out + _dot(pr.astype(BF16), v_ring[slots[b], :, hs])
        ya_ref[:, hs] = (out / denom).astype(ya_ref.dtype)


def _bias_table(rel_bias):
    far = rel_bias[:, 2 * REL_CLIP:]
    head = jnp.broadcast_to(far, (rel_bias.shape[0], LEFT - REL_CLIP))
    body = rel_bias[:, :0:-1]
    tail = jnp.broadcast_to(far, (rel_bias.shape[0],
                                  BIAS_TABLE - (LEFT - REL_CLIP) - 2 * REL_CLIP))
    return jnp.concatenate([head, body, tail], axis=1).astype(F32)


def _row_block_per_step(shape, steps):
    rows, cols = shape
    rep = 1
    while rows % (steps // rep) or (rows // (steps // rep)) % BF16_SUBLANES:
        rep *= 2
    return pl.BlockSpec((rows // (steps // rep), cols), lambda i: (i // rep, 0))


def _mixer(h, gain, w_in, q_gain, k_gain, table, pool_w, pool_scale, round_weights=()):
    s, d = h.shape
    width = ATTN_WIDTH
    tq, nb = MIX_ROWS, MIX_KEY_BLOCKS
    steps = s // tq
    row_block = pl.BlockSpec((tq, width), lambda i: (i, 0))
    cast_specs = [_row_block_per_step(w.shape, steps) for w in round_weights]
    outs = pl.pallas_call(
        _mixer_kernel,
        name="mixer",
        grid=(steps,),
        in_specs=[
            pl.BlockSpec((tq, d), lambda i: (i, 0)),
            _resident((1, d)),
            _resident(w_in.shape),
            _resident((1, HEAD_DIM)),
            _resident((1, HEAD_DIM)),
            _resident(table.shape),
            _resident(pool_w.shape),
            _resident((1, width)),
        ] + cast_specs,
        out_specs=[row_block, row_block] + cast_specs,
        out_shape=[
            jax.ShapeDtypeStruct((s, width), BF16),
            jax.ShapeDtypeStruct((s, width), BF16),
        ] + [jax.ShapeDtypeStruct(w.shape, BF16) for w in round_weights],
        scratch_shapes=[
            pltpu.VMEM((tq, d), BF16),
            pltpu.VMEM((tq, width), BF16),
            pltpu.VMEM((nb, tq, width), BF16),
            pltpu.VMEM((nb, tq, width), BF16),
            pltpu.VMEM((POOL_HALO + tq, width), F32),
            pltpu.VMEM((N_HEADS, tq, MIX_KEYS), F32),
        ],
        compiler_params=_compiler_params(("arbitrary",)),
    )(h, gain, w_in, q_gain, k_gain, table, pool_w, pool_scale, *round_weights)
    return outs[0], outs[1], tuple(outs[2:])


def _merge_kernel(h_ref, yp_ref, ya_ref, gain_ref, wg_ref, bg_ref, wa_ref, wb_ref, wo_ref,
                  o_ref, u_ref):
    d = h_ref.shape[1]
    u_ref[...] = _rms(h_ref[...], gain_ref[...]).astype(BF16)
    g_pool = jax.nn.sigmoid(_dot(u_ref[...], wg_ref[:, :d]) + bg_ref[:, :d])
    merged = g_pool * _dot(yp_ref[...], wa_ref[...])
    g_attn = jax.nn.sigmoid(_dot(u_ref[...], wg_ref[:, d:]) + bg_ref[:, d:])
    merged = merged + g_attn * _dot(ya_ref[...], wb_ref[...])
    o_ref[...] = h_ref[...] + _dot(merged.astype(BF16), wo_ref[...])


def _merge(h, y_pool, y_attn, gain, w_g, b_g, w_a, w_b, w_out):
    s, d = h.shape
    width = y_pool.shape[1]
    tm = MERGE_ROWS
    return pl.pallas_call(
        _merge_kernel,
        name="merge",
        grid=(s // tm,),
        in_specs=[
            pl.BlockSpec((tm, d), lambda i: (i, 0)),
            pl.BlockSpec((tm, width), lambda i: (i, 0)),
            pl.BlockSpec((tm, width), lambda i: (i, 0)),
            _resident((1, d)),
            _resident(w_g.shape),
            _resident(b_g.shape),
            _resident(w_a.shape),
            _resident(w_b.shape),
            _resident(w_out.shape),
        ],
        out_specs=pl.BlockSpec((tm, d), lambda i: (i, 0)),
        out_shape=jax.ShapeDtypeStruct((s, d), F32),
        scratch_shapes=[pltpu.VMEM((tm, d), BF16)],
        compiler_params=_compiler_params(("parallel",)),
    )(h, y_pool, y_attn, gain, w_g, b_g, w_a, w_b, w_out)


def _ple_kernel(h_ref, p_ref, gain_ref, wpg_ref, wple_ref, o_ref):
    t = _rms(h_ref[...], gain_ref[...]).astype(BF16)
    gate = jax.nn.sigmoid(_dot(t, wpg_ref[...]))
    emb = _dot(p_ref[...].astype(BF16), wple_ref[...])
    o_ref[...] = h_ref[...] + gate * emb


def _ple(h, p, gain, w_pg, w_ple):
    s, d = h.shape
    tm = PLE_ROWS
    return pl.pallas_call(
        _ple_kernel,
        name="ple",
        grid=(s // tm,),
        in_specs=[
            pl.BlockSpec((tm, d), lambda i: (i, 0)),
            pl.BlockSpec((tm, p.shape[1]), lambda i: (i, 0)),
            _resident((1, d)),
            _resident(w_pg.shape),
            _resident(w_ple.shape),
        ],
        out_specs=pl.BlockSpec((tm, d), lambda i: (i, 0)),
        out_shape=jax.ShapeDtypeStruct((s, d), F32),
        compiler_params=_compiler_params(("parallel",)),
    )(h, p, gain, w_pg, w_ple)


def kernel(x, p, ffn1_norm, ffn1_w_gate, ffn1_w_up, ffn1_w_down, mix_norm, w_in, pool_w, pool_scale, q_norm, k_norm, rel_bias, w_br_pool, w_br_attn, w_branch_gate, b_branch_gate, w_out, ffn2_norm, ffn2_w_gate, ffn2_w_up, ffn2_w_down, ple_norm, w_ple_gate, w_ple):
    batch, seq, d_model = x.shape
    assert batch == 1, "the row tiling treats the sequence as the only row axis"
    depth = p.shape[0]
    h = x.reshape(seq, d_model)
    for i in range(depth):
        row = lambda a: a[i].reshape(1, -1)
        mm = lambda a: a[i].astype(BF16)
        h_head, ffn1_weights = _ffn_head(h, row(ffn1_norm), ffn1_w_gate[i], ffn1_w_up[i],
                                         ffn1_w_down[i])
        h = _ffn(h, row(ffn1_norm), *ffn1_weights, into=h_head)
        later = (w_branch_gate[i], w_br_pool[i], w_br_attn[i], w_out[i], w_ple_gate[i],
                 ffn2_w_gate[i], ffn2_w_up[i], ffn2_w_down[i])
        y_pool, y_attn, later = _mixer(h, row(mix_norm), mm(w_in), row(q_norm), row(k_norm),
                                       _bias_table(rel_bias[i]), mm(pool_w), row(pool_scale),
                                       round_weights=later)
        w_g, w_a, w_b, w_o, w_pg = later[:5]
        h = _merge(h, y_pool, y_attn, row(mix_norm), w_g, row(b_branch_gate), w_a, w_b, w_o)
        h = _ffn(h, row(ffn2_norm), *later[5:])
        h = _ple(h, p[i].reshape(seq, -1), row(ple_norm), w_pg, mm(w_ple))
    return h.reshape(batch, seq, d_model)
```

```python
import functools
import math

import jax
import jax.numpy as jnp
from jax import lax
from jax.experimental import pallas as pl
from jax.experimental.pallas import tpu as pltpu

F32 = jnp.float32
BF16 = jnp.bfloat16

EPS = 1e-6
MASK_VALUE = -1e30
LOG2E = math.log2(math.e)
CHUNK = 64
LEFT_CHUNKS = 8
LEFT = LEFT_CHUNKS * CHUNK
REL_CLIP = 256
N_HEADS = 8
HEAD_DIM = 128
ATTN_WIDTH = N_HEADS * HEAD_DIM
POOL_WINDOWS = (2, 4, 8, 16)
POOL_GROUP_DIM = 256
POOL_HALO = 16

V7X_VMEM_BYTES = 64 * 1024 * 1024
VMEM_LIMIT_BYTES = V7X_VMEM_BYTES - 6 * 1024 * 1024

FFN_ROWS = 1024
FFN_COLS = 512
FFN_HEAD_TILES = 1
FFN_HEAD_COLS = 256
BF16_SUBLANES = 16
MIX_ROWS = 256
MIX_KEY_BLOCKS = 1 + LEFT // MIX_ROWS
MIX_KEYS = MIX_KEY_BLOCKS * MIX_ROWS
BIAS_TABLE = 1024
MERGE_ROWS = 256
PLE_ROWS = 1024


def _compiler_params(semantics):
    return pltpu.CompilerParams(dimension_semantics=semantics,
                                vmem_limit_bytes=VMEM_LIMIT_BYTES)


def _resident(shape):
    return pl.BlockSpec(shape, lambda *_: (0,) * len(shape), pipeline_mode=pl.Buffered(1))


def _rms(xf, gain):
    return xf * lax.rsqrt(jnp.mean(xf * xf, axis=-1, keepdims=True) + EPS) * gain


def _dot(a, b):
    return jnp.dot(a, b, preferred_element_type=F32)


def _ffn_kernel(x_ref, gain_ref, wg_ref, wu_ref, wd_ref, *refs, round_weights, aliased, n_extra):
    refs = refs[1:] if aliased else refs
    extra_in, refs = refs[:n_extra], refs[n_extra:]
    o_ref, xn_ref = refs[0], refs[-1]
    if round_weights:
        for src, dst in zip((wg_ref, wu_ref, wd_ref), refs[1:4]):
            dst[...] = src[...].astype(dst.dtype)
        wg_ref, wu_ref, wd_ref = refs[1:4]
    for src, dst in zip(extra_in, refs[-1 - n_extra:-1]):
        dst[...] = src[...].astype(dst.dtype)

    @pl.when(pl.program_id(1) == 0)
    def _():
        xf = x_ref[...]
        xn_ref[...] = _rms(xf, gain_ref[...]).astype(BF16)
        o_ref[...] = xf

    xn = xn_ref[...]
    g = _dot(xn, wg_ref[...])
    u = _dot(xn, wu_ref[...])
    h = (g * jax.nn.sigmoid(g) * u * 0.5).astype(BF16)
    o_ref[...] += _dot(h, wd_ref[...])


def _ffn_specs(d, tm, tf, first_tile):
    row_block = pl.BlockSpec((tm, d), lambda i, j: (i + first_tile, 0))
    weight_specs = [
        pl.BlockSpec((d, tf), lambda i, j: (0, j)),
        pl.BlockSpec((d, tf), lambda i, j: (0, j)),
        pl.BlockSpec((tf, d), lambda i, j: (j, 0)),
    ]
    return row_block, [row_block, pl.BlockSpec((1, d), lambda i, j: (0, 0))] + weight_specs, weight_specs


def _ffn_head(x, gain, wg32, wu32, wd32):
    s, d = x.shape
    f = wg32.shape[1]
    tm, tf = FFN_ROWS, FFN_HEAD_COLS
    row_block, in_specs, weight_specs = _ffn_specs(d, tm, tf, 0)
    outs = pl.pallas_call(
        functools.partial(_ffn_kernel, round_weights=True, aliased=False, n_extra=0),
        name="ffn_head",
        grid=(FFN_HEAD_TILES, f // tf),
        in_specs=in_specs,
        out_specs=[row_block] + weight_specs,
        out_shape=[jax.ShapeDtypeStruct((s, d), F32)]
        + [jax.ShapeDtypeStruct(w.shape, BF16) for w in (wg32, wu32, wd32)],
        scratch_shapes=[pltpu.VMEM((tm, d), BF16)],
        compiler_params=_compiler_params(("arbitrary", "arbitrary")),
    )(x, gain, wg32, wu32, wd32)
    return outs[0], tuple(outs[1:])


def _flat_step_row_block(shape, grid):
    rows, cols = shape
    n_blocks = rows // BF16_SUBLANES
    assert rows % BF16_SUBLANES == 0 and n_blocks <= grid[0] * grid[1]
    return pl.BlockSpec((BF16_SUBLANES, cols),
                        lambda i, j: (jnp.minimum(i * grid[1] + j, n_blocks - 1), 0))


def _ffn(x, gain, wg, wu, wd, into=None, round_weights=()):
    s, d = x.shape
    f = wg.shape[1]
    tm, tf = FFN_ROWS, FFN_COLS
    first_tile = 0 if into is None else FFN_HEAD_TILES
    grid = (s // tm - first_tile, f // tf)
    row_block, in_specs, _ = _ffn_specs(d, tm, tf, first_tile)
    operands = [x, gain, wg, wu, wd]
    aliases = {}
    if into is not None:
        in_specs.append(pl.BlockSpec(memory_space=pl.ANY))
        aliases = {len(operands): 0}
        operands.append(into)
    extra_specs = [_flat_step_row_block(w.shape, grid) for w in round_weights]
    outs = pl.pallas_call(
        functools.partial(_ffn_kernel, round_weights=False, aliased=into is not None,
                          n_extra=len(round_weights)),
        name="ffn",
        grid=grid,
        in_specs=in_specs + extra_specs,
        out_specs=[row_block] + extra_specs,
        out_shape=[jax.ShapeDtypeStruct((s, d), F32)]
        + [jax.ShapeDtypeStruct(w.shape, BF16) for w in round_weights],
        input_output_aliases=aliases,
        scratch_shapes=[pltpu.VMEM((tm, d), BF16)],
        compiler_params=_compiler_params(
            ("arbitrary" if round_weights else "parallel", "arbitrary")),
    )(*operands, *round_weights)
    return outs[0], tuple(outs[1:])


def _head_norm_store(acc, gain, scale, out_ref):
    for h in range(N_HEADS):
        sl = slice(h * HEAD_DIM, (h + 1) * HEAD_DIM)
        out_ref[:, sl] = (_rms(acc[:, sl], gain) * scale).astype(out_ref.dtype)


def _build_attention_bias(i, table_ref, bias_ref):
    tq, tk = MIX_ROWS, MIX_KEYS
    r = lax.broadcasted_iota(jnp.int32, (tq, tk), 0)
    c = lax.broadcasted_iota(jnp.int32, (tq, tk), 1)
    q_chunk = r // CHUNK
    k_chunk = c // CHUNK
    first_key = (MIX_KEY_BLOCKS - 1 - i) * tq
    visible = (k_chunk >= q_chunk) & (k_chunk <= q_chunk + LEFT_CHUNKS) & (c >= first_key)
    for h in range(N_HEADS):
        rows = jnp.broadcast_to(table_ref[h:h + 1, :], (tq, BIAS_TABLE))
        rolled = pltpu.roll(rows, 0, 1, stride=1, stride_axis=0)
        bias_ref[h] = jnp.where(visible, rolled[:, :tk] * LOG2E, MASK_VALUE)


def _window_sums(z, w):
    span = 1
    while span < w:
        z = z + pltpu.roll(z, span, 0)
        span *= 2
    return z


def _mixer_kernel(h_ref, gain_ref, w_ref, qg_ref, kg_ref, table_ref, pw_ref, ps_ref, *refs):
    n_cast = (len(refs) - 8) // 2
    cast_in, (yp_ref, ya_ref, u_ref), cast_out = (refs[:n_cast], refs[n_cast:n_cast + 3],
                                                  refs[n_cast + 3:-5])
    q_ref, k_ring, v_ring, zs_ref, bias_ref = refs[-5:]
    nb = MIX_KEY_BLOCKS
    tq = MIX_ROWS
    width = ATTN_WIDTH
    i = pl.program_id(0)

    @pl.when(i < nb)
    def _():
        _build_attention_bias(i, table_ref, bias_ref)

    @pl.when(i == 0)
    def _():
        k_ring[...] = jnp.zeros_like(k_ring)
        v_ring[...] = jnp.zeros_like(v_ring)
        zs_ref[:POOL_HALO, :] = jnp.zeros((POOL_HALO, width), F32)

    @pl.when(i > 0)
    def _():
        zs_ref[:POOL_HALO, :] = zs_ref[tq:, :]

    def round_weights(part, parts=4):
        for src, dst in list(zip(cast_in, cast_out))[part::parts]:
            dst[...] = src[...].astype(dst.dtype)

    slot = lax.rem(i, nb)
    u_ref[...] = _rms(h_ref[...], gain_ref[...]).astype(BF16)
    zs_ref[POOL_HALO:, :] = _dot(u_ref[...], w_ref[:, 0:width])
    round_weights(0)
    _head_norm_store(_dot(u_ref[...], w_ref[:, width:2 * width]),
                     qg_ref[...], HEAD_DIM ** -0.5 * LOG2E, q_ref)
    round_weights(1)
    _head_norm_store(_dot(u_ref[...], w_ref[:, 2 * width:3 * width]),
                     kg_ref[...], 1.0, k_ring.at[slot])
    round_weights(2)
    v_ring[slot] = _dot(u_ref[...], w_ref[:, 3 * width:4 * width]).astype(BF16)
    round_weights(3)

    t = i * tq + lax.broadcasted_iota(jnp.int32, (tq, 1), 0)
    for g, w in enumerate(POOL_WINDOWS):
        cols = slice(g * POOL_GROUP_DIM, (g + 1) * POOL_GROUP_DIM)
        z = zs_ref[:, cols]
        inv_count = 1.0 / jnp.minimum(t + 1, w).astype(F32)
        diff = (_window_sums(z, w)[POOL_HALO:] * inv_count - z[POOL_HALO:]).astype(BF16)
        y = _dot(diff, pw_ref[g]) * ps_ref[:, cols]
        yp_ref[:, cols] = y.astype(yp_ref.dtype)

    slots = [lax.rem(i + 1 + b, nb) for b in range(nb)]
    for h in range(N_HEADS):
        hs = slice(h * HEAD_DIM, (h + 1) * HEAD_DIM)
        qh = q_ref[:, hs]
        scores = []
        for b in range(nb):
            sc = lax.dot_general(qh, k_ring[slots[b], :, hs], (((1,), (1,)), ((), ())),
                                 preferred_element_type=F32)
            scores.append(sc + bias_ref[h, :, b * tq:(b + 1) * tq])
        m = scores[0].max(axis=-1, keepdims=True)
        for b in range(1, nb):
            m = jnp.maximum(m, scores[b].max(axis=-1, keepdims=True))
        denom = jnp.zeros((tq, 1), F32)
        out = jnp.zeros((tq, HEAD_DIM), F32)
        for b in range(nb):
            pr = jnp.exp2(scores[b] - m)
            denom = denom + pr.sum(axis=-1, keepdims=True)
            out = out + _dot(pr.astype(BF16), v_ring[slots[b], :, hs])
        ya_ref[:, hs] = (out / denom).astype(ya_ref.dtype)


def _bias_table(rel_bias):
    far = rel_bias[:, 2 * REL_CLIP:]
    head = jnp.broadcast_to(far, (rel_bias.shape[0], LEFT - REL_CLIP))
    body = rel_bias[:, :0:-1]
    tail = jnp.broadcast_to(far, (rel_bias.shape[0],
                                  BIAS_TABLE - (LEFT - REL_CLIP) - 2 * REL_CLIP))
    return jnp.concatenate([head, body, tail], axis=1).astype(F32)


def _row_block_per_step(shape, steps):
    rows, cols = shape
    rep = 1
    while rows % (steps // rep) or (rows // (steps // rep)) % BF16_SUBLANES:
        rep *= 2
    return pl.BlockSpec((rows // (steps // rep), cols), lambda i: (i // rep, 0))


def _mixer(h, gain, w_in, q_gain, k_gain, table, pool_w, pool_scale, round_weights=()):
    s, d = h.shape
    width = ATTN_WIDTH
    tq, nb = MIX_ROWS, MIX_KEY_BLOCKS
    steps = s // tq
    row_block = pl.BlockSpec((tq, width), lambda i: (i, 0))
    cast_specs = [_row_block_per_step(w.shape, steps) for w in round_weights]
    outs = pl.pallas_call(
        _mixer_kernel,
        name="mixer",
        grid=(steps,),
        in_specs=[
            pl.BlockSpec((tq, d), lambda i: (i, 0)),
            _resident((1, d)),
            _resident(w_in.shape),
            _resident((1, HEAD_DIM)),
            _resident((1, HEAD_DIM)),
            _resident(table.shape),
            _resident(pool_w.shape),
            _resident((1, width)),
        ] + cast_specs,
        out_specs=[row_block, row_block, pl.BlockSpec((tq, d), lambda i: (i, 0))] + cast_specs,
        out_shape=[
            jax.ShapeDtypeStruct((s, width), BF16),
            jax.ShapeDtypeStruct((s, width), BF16),
            jax.ShapeDtypeStruct((s, d), BF16),
        ] + [jax.ShapeDtypeStruct(w.shape, BF16) for w in round_weights],
        scratch_shapes=[
            pltpu.VMEM((tq, width), BF16),
            pltpu.VMEM((nb, tq, width), BF16),
            pltpu.VMEM((nb, tq, width), BF16),
            pltpu.VMEM((POOL_HALO + tq, width), F32),
            pltpu.VMEM((N_HEADS, tq, MIX_KEYS), F32),
        ],
        compiler_params=_compiler_params(("arbitrary",)),
    )(h, gain, w_in, q_gain, k_gain, table, pool_w, pool_scale, *round_weights)
    return outs[0], outs[1], outs[2], tuple(outs[3:])


def _merge_kernel(h_ref, u_ref, yp_ref, ya_ref, wg_ref, bg_ref, wa_ref, wb_ref, wo_ref, o_ref):
    d = h_ref.shape[1]
    g_pool = jax.nn.sigmoid(_dot(u_ref[...], wg_ref[:, :d]) + bg_ref[:, :d])
    merged = g_pool * _dot(yp_ref[...], wa_ref[...])
    g_attn = jax.nn.sigmoid(_dot(u_ref[...], wg_ref[:, d:]) + bg_ref[:, d:])
    merged = merged + g_attn * _dot(ya_ref[...], wb_ref[...])
    o_ref[...] = h_ref[...] + _dot(merged.astype(BF16), wo_ref[...])


def _merge(h, u, y_pool, y_attn, w_g, b_g, w_a, w_b, w_out):
    s, d = h.shape
    width = y_pool.shape[1]
    tm = MERGE_ROWS
    return pl.pallas_call(
        _merge_kernel,
        name="merge",
        grid=(s // tm,),
        in_specs=[
            pl.BlockSpec((tm, d), lambda i: (i, 0)),
            pl.BlockSpec((tm, d), lambda i: (i, 0)),
            pl.BlockSpec((tm, width), lambda i: (i, 0)),
            pl.BlockSpec((tm, width), lambda i: (i, 0)),
            _resident(w_g.shape),
            _resident(b_g.shape),
            _resident(w_a.shape),
            _resident(w_b.shape),
            _resident(w_out.shape),
        ],
        out_specs=pl.BlockSpec((tm, d), lambda i: (i, 0)),
        out_shape=jax.ShapeDtypeStruct((s, d), F32),
        compiler_params=_compiler_params(("parallel",)),
    )(h, u, y_pool, y_attn, w_g, b_g, w_a, w_b, w_out)


def _ple_kernel(h_ref, p_ref, gain_ref, wpg_ref, wple_ref, o_ref):
    t = _rms(h_ref[...], gain_ref[...]).astype(BF16)
    gate = jax.nn.sigmoid(_dot(t, wpg_ref[...]))
    emb = _dot(p_ref[...].astype(BF16), wple_ref[...])
    o_ref[...] = h_ref[...] + gate * emb


def _ple(h, p, gain, w_pg, w_ple):
    s, d = h.shape
    tm = PLE_ROWS
    return pl.pallas_call(
        _ple_kernel,
        name="ple",
        grid=(s // tm,),
        in_specs=[
            pl.BlockSpec((tm, d), lambda i: (i, 0)),
            pl.BlockSpec((tm, p.shape[1]), lambda i: (i, 0)),
            _resident((1, d)),
            _resident(w_pg.shape),
            _resident(w_ple.shape),
        ],
        out_specs=pl.BlockSpec((tm, d), lambda i: (i, 0)),
        out_shape=jax.ShapeDtypeStruct((s, d), F32),
        compiler_params=_compiler_params(("parallel",)),
    )(h, p, gain, w_pg, w_ple)


def kernel(x, p, ffn1_norm, ffn1_w_gate, ffn1_w_up, ffn1_w_down, mix_norm, w_in, pool_w, pool_scale, q_norm, k_norm, rel_bias, w_br_pool, w_br_attn, w_branch_gate, b_branch_gate, w_out, ffn2_norm, ffn2_w_gate, ffn2_w_up, ffn2_w_down, ple_norm, w_ple_gate, w_ple):
    batch, seq, d_model = x.shape
    assert batch == 1, "the row tiling treats the sequence as the only row axis"
    depth = p.shape[0]
    h = x.reshape(seq, d_model)
    for i in range(depth):
        row = lambda a: a[i].reshape(1, -1)
        mm = lambda a: a[i].astype(BF16)
        h_head, ffn1_weights = _ffn_head(h, row(ffn1_norm), ffn1_w_gate[i], ffn1_w_up[i],
                                         ffn1_w_down[i])
        h, (w_in16,) = _ffn(h, row(ffn1_norm), *ffn1_weights, into=h_head,
                            round_weights=(w_in[i],))
        later = (w_branch_gate[i], w_br_pool[i], w_br_attn[i], w_out[i], w_ple_gate[i],
                 ffn2_w_gate[i], ffn2_w_up[i], ffn2_w_down[i])
        y_pool, y_attn, u, later = _mixer(h, row(mix_norm), w_in16, row(q_norm), row(k_norm),
                                          _bias_table(rel_bias[i]), mm(pool_w), row(pool_scale),
                                          round_weights=later)
        w_g, w_a, w_b, w_o, w_pg = later[:5]
        h = _merge(h, u, y_pool, y_attn, w_g, row(b_branch_gate), w_a, w_b, w_o)
        h, _ = _ffn(h, row(ffn2_norm), *later[5:])
        h = _ple(h, p[i].reshape(seq, -1), row(ple_norm), w_pg, mm(w_ple))
    return h.reshape(batch, seq, d_model)
```

```python
import functools
import math

import jax
import jax.numpy as jnp
from jax import lax
from jax.experimental import pallas as pl
from jax.experimental.pallas import tpu as pltpu

F32 = jnp.float32
BF16 = jnp.bfloat16

EPS = 1e-6
MASK_VALUE = -1e30
LOG2E = math.log2(math.e)
CHUNK = 64
LEFT_CHUNKS = 8
LEFT = LEFT_CHUNKS * CHUNK
REL_CLIP = 256
N_HEADS = 8
HEAD_DIM = 128
ATTN_WIDTH = N_HEADS * HEAD_DIM
POOL_WINDOWS = (2, 4, 8, 16)
POOL_GROUP_DIM = 256
POOL_HALO = 16

V7X_VMEM_BYTES = 64 * 1024 * 1024
VMEM_LIMIT_BYTES = V7X_VMEM_BYTES - 6 * 1024 * 1024

FFN_ROWS = 1024
FFN_COLS = 512
FFN_HEAD_TILES = 1
FFN_HEAD_COLS = 256
BF16_SUBLANES = 16
MIX_ROWS = 256
MIX_KEY_BLOCKS = 1 + LEFT // MIX_ROWS
MIX_KEYS = MIX_KEY_BLOCKS * MIX_ROWS
MIX_HEADS_AHEAD = 1
BIAS_TABLE = 1024
MERGE_ROWS = 256
PLE_ROWS = 1024


def _compiler_params(semantics):
    return pltpu.CompilerParams(dimension_semantics=semantics,
                                vmem_limit_bytes=VMEM_LIMIT_BYTES)


def _resident(shape):
    return pl.BlockSpec(shape, lambda *_: (0,) * len(shape), pipeline_mode=pl.Buffered(1))


def _rms(xf, gain):
    return xf * lax.rsqrt(jnp.mean(xf * xf, axis=-1, keepdims=True) + EPS) * gain


def _dot(a, b):
    return jnp.dot(a, b, preferred_element_type=F32)


def _ffn_kernel(x_ref, gain_ref, wg_ref, wu_ref, wd_ref, *refs, round_weights, aliased, n_extra):
    refs = refs[1:] if aliased else refs
    extra_in, refs = refs[:n_extra], refs[n_extra:]
    o_ref, xn_ref = refs[0], refs[-1]
    if round_weights:
        for src, dst in zip((wg_ref, wu_ref, wd_ref), refs[1:4]):
            dst[...] = src[...].astype(dst.dtype)
        wg_ref, wu_ref, wd_ref = refs[1:4]
    for src, dst in zip(extra_in, refs[-1 - n_extra:-1]):
        dst[...] = src[...].astype(dst.dtype)

    @pl.when(pl.program_id(1) == 0)
    def _():
        xf = x_ref[...]
        xn_ref[...] = _rms(xf, gain_ref[...]).astype(BF16)
        o_ref[...] = xf

    xn = xn_ref[...]
    g = _dot(xn, wg_ref[...])
    u = _dot(xn, wu_ref[...])
    h = (g * jax.nn.sigmoid(g) * u * 0.5).astype(BF16)
    o_ref[...] += _dot(h, wd_ref[...])


def _ffn_specs(d, tm, tf, first_tile):
    row_block = pl.BlockSpec((tm, d), lambda i, j: (i + first_tile, 0))
    weight_specs = [
        pl.BlockSpec((d, tf), lambda i, j: (0, j)),
        pl.BlockSpec((d, tf), lambda i, j: (0, j)),
        pl.BlockSpec((tf, d), lambda i, j: (j, 0)),
    ]
    return row_block, [row_block, pl.BlockSpec((1, d), lambda i, j: (0, 0))] + weight_specs, weight_specs


def _ffn_head(x, gain, wg32, wu32, wd32):
    s, d = x.shape
    f = wg32.shape[1]
    tm, tf = FFN_ROWS, FFN_HEAD_COLS
    row_block, in_specs, weight_specs = _ffn_specs(d, tm, tf, 0)
    outs = pl.pallas_call(
        functools.partial(_ffn_kernel, round_weights=True, aliased=False, n_extra=0),
        name="ffn_head",
        grid=(FFN_HEAD_TILES, f // tf),
        in_specs=in_specs,
        out_specs=[row_block] + weight_specs,
        out_shape=[jax.ShapeDtypeStruct((s, d), F32)]
        + [jax.ShapeDtypeStruct(w.shape, BF16) for w in (wg32, wu32, wd32)],
        scratch_shapes=[pltpu.VMEM((tm, d), BF16)],
        compiler_params=_compiler_params(("arbitrary", "arbitrary")),
    )(x, gain, wg32, wu32, wd32)
    return outs[0], tuple(outs[1:])


def _flat_step_row_block(shape, grid):
    rows, cols = shape
    n_blocks = rows // BF16_SUBLANES
    assert rows % BF16_SUBLANES == 0 and n_blocks <= grid[0] * grid[1]
    return pl.BlockSpec((BF16_SUBLANES, cols),
                        lambda i, j: (jnp.minimum(i * grid[1] + j, n_blocks - 1), 0))


def _ffn(x, gain, wg, wu, wd, into=None, round_weights=()):
    s, d = x.shape
    f = wg.shape[1]
    tm, tf = FFN_ROWS, FFN_COLS
    first_tile = 0 if into is None else FFN_HEAD_TILES
    grid = (s // tm - first_tile, f // tf)
    row_block, in_specs, _ = _ffn_specs(d, tm, tf, first_tile)
    operands = [x, gain, wg, wu, wd]
    aliases = {}
    if into is not None:
        in_specs.append(pl.BlockSpec(memory_space=pl.ANY))
        aliases = {len(operands): 0}
        operands.append(into)
    extra_specs = [_flat_step_row_block(w.shape, grid) for w in round_weights]
    outs = pl.pallas_call(
        functools.partial(_ffn_kernel, round_weights=False, aliased=into is not None,
                          n_extra=len(round_weights)),
        name="ffn",
        grid=grid,
        in_specs=in_specs + extra_specs,
        out_specs=[row_block] + extra_specs,
        out_shape=[jax.ShapeDtypeStruct((s, d), F32)]
        + [jax.ShapeDtypeStruct(w.shape, BF16) for w in round_weights],
        input_output_aliases=aliases,
        scratch_shapes=[pltpu.VMEM((tm, d), BF16)],
        compiler_params=_compiler_params(
            ("arbitrary" if round_weights else "parallel", "arbitrary")),
    )(*operands, *round_weights)
    return outs[0], tuple(outs[1:])


def _head_norm_store(acc, gain, scale, out_ref):
    for h in range(N_HEADS):
        sl = slice(h * HEAD_DIM, (h + 1) * HEAD_DIM)
        out_ref[:, sl] = (_rms(acc[:, sl], gain) * scale).astype(out_ref.dtype)


def _build_attention_bias(i, table_ref, bias_ref):
    tq, tk = MIX_ROWS, MIX_KEYS
    r = lax.broadcasted_iota(jnp.int32, (tq, tk), 0)
    c = lax.broadcasted_iota(jnp.int32, (tq, tk), 1)
    q_chunk = r // CHUNK
    k_chunk = c // CHUNK
    first_key = (MIX_KEY_BLOCKS - 1 - i) * tq
    visible = (k_chunk >= q_chunk) & (k_chunk <= q_chunk + LEFT_CHUNKS) & (c >= first_key)
    for h in range(N_HEADS):
        rows = jnp.broadcast_to(table_ref[h:h + 1, :], (tq, BIAS_TABLE))
        rolled = pltpu.roll(rows, 0, 1, stride=1, stride_axis=0)
        bias_ref[h] = jnp.where(visible, rolled[:, :tk] * LOG2E, MASK_VALUE)


def _window_sums(z, w):
    span = 1
    while span < w:
        z = z + pltpu.roll(z, span, 0)
        span *= 2
    return z


def _mixer_kernel(h_ref, gain_ref, w_ref, qg_ref, kg_ref, table_ref, pw_ref, ps_ref, *refs):
    n_cast = (len(refs) - 8) // 2
    cast_in, (yp_ref, ya_ref, u_ref), cast_out = (refs[:n_cast], refs[n_cast:n_cast + 3],
                                                  refs[n_cast + 3:-5])
    q_ref, k_ring, v_ring, zs_ref, bias_ref = refs[-5:]
    nb = MIX_KEY_BLOCKS
    tq = MIX_ROWS
    width = ATTN_WIDTH
    i = pl.program_id(0)

    @pl.when(i < nb)
    def _():
        _build_attention_bias(i, table_ref, bias_ref)

    @pl.when(i == 0)
    def _():
        k_ring[...] = jnp.zeros_like(k_ring)
        v_ring[...] = jnp.zeros_like(v_ring)
        zs_ref[:POOL_HALO, :] = jnp.zeros((POOL_HALO, width), F32)

    @pl.when(i > 0)
    def _():
        zs_ref[:POOL_HALO, :] = zs_ref[tq:, :]

    def round_weights(part, parts=4):
        for src, dst in list(zip(cast_in, cast_out))[part::parts]:
            dst[...] = src[...].astype(dst.dtype)

    slot = lax.rem(i, nb)
    u_ref[...] = _rms(h_ref[...], gain_ref[...]).astype(BF16)
    zs_ref[POOL_HALO:, :] = _dot(u_ref[...], w_ref[:, 0:width])
    round_weights(0)
    _head_norm_store(_dot(u_ref[...], w_ref[:, width:2 * width]),
                     qg_ref[...], HEAD_DIM ** -0.5 * LOG2E, q_ref)
    round_weights(1)
    _head_norm_store(_dot(u_ref[...], w_ref[:, 2 * width:3 * width]),
                     kg_ref[...], 1.0, k_ring.at[slot])
    round_weights(2)
    v_ring[slot] = _dot(u_ref[...], w_ref[:, 3 * width:4 * width]).astype(BF16)
    round_weights(3)

    t = i * tq + lax.broadcasted_iota(jnp.int32, (tq, 1), 0)
    for g, w in enumerate(POOL_WINDOWS):
        cols = slice(g * POOL_GROUP_DIM, (g + 1) * POOL_GROUP_DIM)
        z = zs_ref[:, cols]
        inv_count = 1.0 / jnp.minimum(t + 1, w).astype(F32)
        diff = (_window_sums(z, w)[POOL_HALO:] * inv_count - z[POOL_HALO:]).astype(BF16)
        y = _dot(diff, pw_ref[g]) * ps_ref[:, cols]
        yp_ref[:, cols] = y.astype(yp_ref.dtype)

    slots = [lax.rem(i + 1 + b, nb) for b in range(nb)]

    def head_scores(h):
        hs = slice(h * HEAD_DIM, (h + 1) * HEAD_DIM)
        qh = q_ref[:, hs]
        return [lax.dot_general(qh, k_ring[slots[b], :, hs], (((1,), (1,)), ((), ())),
                                preferred_element_type=F32) + bias_ref[h, :, b * tq:(b + 1) * tq]
                for b in range(nb)]

    def head_probs(scores):
        m = scores[0].max(axis=-1, keepdims=True)
        for b in range(1, nb):
            m = jnp.maximum(m, scores[b].max(axis=-1, keepdims=True))
        probs = [jnp.exp2(sc - m) for sc in scores]
        denom = probs[0].sum(axis=-1, keepdims=True)
        for pr in probs[1:]:
            denom = denom + pr.sum(axis=-1, keepdims=True)
        return [pr.astype(BF16) for pr in probs], denom

    def head_output(h, probs, denom):
        hs = slice(h * HEAD_DIM, (h + 1) * HEAD_DIM)
        out = _dot(probs[0], v_ring[slots[0], :, hs])
        for b in range(1, nb):
            out = out + _dot(probs[b], v_ring[slots[b], :, hs])
        ya_ref[:, hs] = (out / denom).astype(ya_ref.dtype)

    ahead = MIX_HEADS_AHEAD
    pending = {h: head_scores(h) for h in range(ahead)}
    for h in range(N_HEADS):
        if h + ahead < N_HEADS:
            pending[h + ahead] = head_scores(h + ahead)
        head_output(h, *head_probs(pending.pop(h)))


def _bias_table(rel_bias):
    far = rel_bias[:, 2 * REL_CLIP:]
    head = jnp.broadcast_to(far, (rel_bias.shape[0], LEFT - REL_CLIP))
    body = rel_bias[:, :0:-1]
    tail = jnp.broadcast_to(far, (rel_bias.shape[0],
                                  BIAS_TABLE - (LEFT - REL_CLIP) - 2 * REL_CLIP))
    return jnp.concatenate([head, body, tail], axis=1).astype(F32)


def _row_block_per_step(shape, steps):
    rows, cols = shape
    rep = 1
    while rows % (steps // rep) or (rows // (steps // rep)) % BF16_SUBLANES:
        rep *= 2
    return pl.BlockSpec((rows // (steps // rep), cols), lambda i: (i // rep, 0))


def _mixer(h, gain, w_in, q_gain, k_gain, table, pool_w, pool_scale, round_weights=()):
    s, d = h.shape
    width = ATTN_WIDTH
    tq, nb = MIX_ROWS, MIX_KEY_BLOCKS
    steps = s // tq
    row_block = pl.BlockSpec((tq, width), lambda i: (i, 0))
    cast_specs = [_row_block_per_step(w.shape, steps) for w in round_weights]
    outs = pl.pallas_call(
        _mixer_kernel,
        name="mixer",
        grid=(steps,),
        in_specs=[
            pl.BlockSpec((tq, d), lambda i: (i, 0)),
            _resident((1, d)),
            _resident(w_in.shape),
            _resident((1, HEAD_DIM)),
            _resident((1, HEAD_DIM)),
            _resident(table.shape),
            _resident(pool_w.shape),
            _resident((1, width)),
        ] + cast_specs,
        out_specs=[row_block, row_block, pl.BlockSpec((tq, d), lambda i: (i, 0))] + cast_specs,
        out_shape=[
            jax.ShapeDtypeStruct((s, width), BF16),
            jax.ShapeDtypeStruct((s, width), BF16),
            jax.ShapeDtypeStruct((s, d), BF16),
        ] + [jax.ShapeDtypeStruct(w.shape, BF16) for w in round_weights],
        scratch_shapes=[
            pltpu.VMEM((tq, width), BF16),
            pltpu.VMEM((nb, tq, width), BF16),
            pltpu.VMEM((nb, tq, width), BF16),
            pltpu.VMEM((POOL_HALO + tq, width), F32),
            pltpu.VMEM((N_HEADS, tq, MIX_KEYS), F32),
        ],
        compiler_params=_compiler_params(("arbitrary",)),
    )(h, gain, w_in, q_gain, k_gain, table, pool_w, pool_scale, *round_weights)
    return outs[0], outs[1], outs[2], tuple(outs[3:])


def _merge_kernel(h_ref, u_ref, yp_ref, ya_ref, wg_ref, bg_ref, wa_ref, wb_ref, wo_ref, o_ref):
    d = h_ref.shape[1]
    g_pool = jax.nn.sigmoid(_dot(u_ref[...], wg_ref[:, :d]) + bg_ref[:, :d])
    merged = g_pool * _dot(yp_ref[...], wa_ref[...])
    g_attn = jax.nn.sigmoid(_dot(u_ref[...], wg_ref[:, d:]) + bg_ref[:, d:])
    merged = merged + g_attn * _dot(ya_ref[...], wb_ref[...])
    o_ref[...] = h_ref[...] + _dot(merged.astype(BF16), wo_ref[...])


def _merge(h, u, y_pool, y_attn, w_g, b_g, w_a, w_b, w_out):
    s, d = h.shape
    width = y_pool.shape[1]
    tm = MERGE_ROWS
    return pl.pallas_call(
        _merge_kernel,
        name="merge",
        grid=(s // tm,),
        in_specs=[
            pl.BlockSpec((tm, d), lambda i: (i, 0)),
            pl.BlockSpec((tm, d), lambda i: (i, 0)),
            pl.BlockSpec((tm, width), lambda i: (i, 0)),
            pl.BlockSpec((tm, width), lambda i: (i, 0)),
            _resident(w_g.shape),
            _resident(b_g.shape),
            _resident(w_a.shape),
            _resident(w_b.shape),
            _resident(w_out.shape),
        ],
        out_specs=pl.BlockSpec((tm, d), lambda i: (i, 0)),
        out_shape=jax.ShapeDtypeStruct((s, d), F32),
        compiler_params=_compiler_params(("parallel",)),
    )(h, u, y_pool, y_attn, w_g, b_g, w_a, w_b, w_out)


def _ple_kernel(h_ref, p_ref, gain_ref, wpg_ref, wple_ref, o_ref):
    t = _rms(h_ref[...], gain_ref[...]).astype(BF16)
    gate = jax.nn.sigmoid(_dot(t, wpg_ref[...]))
    emb = _dot(p_ref[...].astype(BF16), wple_ref[...])
    o_ref[...] = h_ref[...] + gate * emb


def _ple(h, p, gain, w_pg, w_ple):
    s, d = h.shape
    tm = PLE_ROWS
    return pl.pallas_call(
        _ple_kernel,
        name="ple",
        grid=(s // tm,),
        in_specs=[
            pl.BlockSpec((tm, d), lambda i: (i, 0)),
            pl.BlockSpec((tm, p.shape[1]), lambda i: (i, 0)),
            _resident((1, d)),
            _resident(w_pg.shape),
            _resident(w_ple.shape),
        ],
        out_specs=pl.BlockSpec((tm, d), lambda i: (i, 0)),
        out_shape=jax.ShapeDtypeStruct((s, d), F32),
        compiler_params=_compiler_params(("parallel",)),
    )(h, p, gain, w_pg, w_ple)


def kernel(x, p, ffn1_norm, ffn1_w_gate, ffn1_w_up, ffn1_w_down, mix_norm, w_in, pool_w, pool_scale, q_norm, k_norm, rel_bias, w_br_pool, w_br_attn, w_branch_gate, b_branch_gate, w_out, ffn2_norm, ffn2_w_gate, ffn2_w_up, ffn2_w_down, ple_norm, w_ple_gate, w_ple):
    batch, seq, d_model = x.shape
    assert batch == 1, "the row tiling treats the sequence as the only row axis"
    depth = p.shape[0]
    h = x.reshape(seq, d_model)
    for i in range(depth):
        row = lambda a: a[i].reshape(1, -1)
        mm = lambda a: a[i].astype(BF16)
        h_head, ffn1_weights = _ffn_head(h, row(ffn1_norm), ffn1_w_gate[i], ffn1_w_up[i],
                                         ffn1_w_down[i])
        h, (w_in16,) = _ffn(h, row(ffn1_norm), *ffn1_weights, into=h_head,
                            round_weights=(w_in[i],))
        later = (w_branch_gate[i], w_br_pool[i], w_br_attn[i], w_out[i], w_ple_gate[i],
                 ffn2_w_gate[i], ffn2_w_up[i], ffn2_w_down[i])
        y_pool, y_attn, u, later = _mixer(h, row(mix_norm), w_in16, row(q_norm), row(k_norm),
                                          _bias_table(rel_bias[i]), mm(pool_w), row(pool_scale),
                                          round_weights=later)
        w_g, w_a, w_b, w_o, w_pg = later[:5]
        h = _merge(h, u, y_pool, y_attn, w_g, row(b_branch_gate), w_a, w_b, w_o)
        h, _ = _ffn(h, row(ffn2_norm), *later[5:])
        h = _ple(h, p[i].reshape(seq, -1), row(ple_norm), w_pg, mm(w_ple))
    return h.reshape(batch, seq, d_model)
```

```python
import functools
import math

import jax
import jax.numpy as jnp
from jax import lax
from jax.experimental import pallas as pl
from jax.experimental.pallas import tpu as pltpu

F32 = jnp.float32
BF16 = jnp.bfloat16

EPS = 1e-6
MASK_VALUE = -1e30
LOG2E = math.log2(math.e)
CHUNK = 64
LEFT_CHUNKS = 8
LEFT = LEFT_CHUNKS * CHUNK
REL_CLIP = 256
N_HEADS = 8
HEAD_DIM = 128
ATTN_WIDTH = N_HEADS * HEAD_DIM
POOL_WINDOWS = (2, 4, 8, 16)
POOL_GROUP_DIM = 256
POOL_HALO = 16

V7X_VMEM_BYTES = 64 * 1024 * 1024
VMEM_LIMIT_BYTES = V7X_VMEM_BYTES - 6 * 1024 * 1024

FFN_ROWS = 1024
FFN_COLS = 512
FFN_HEAD_TILES = 1
FFN_HEAD_COLS = 256
BF16_SUBLANES = 16
MIX_ROWS = 256
MIX_KEY_BLOCKS = 1 + LEFT // MIX_ROWS
MIX_KEYS = MIX_KEY_BLOCKS * MIX_ROWS
MIX_HEADS_AHEAD = 1
BIAS_TABLE = 1024
MERGE_ROWS = 256
PLE_ROWS = 1024


def _compiler_params(semantics):
    return pltpu.CompilerParams(dimension_semantics=semantics,
                                vmem_limit_bytes=VMEM_LIMIT_BYTES)


def _resident(shape):
    return pl.BlockSpec(shape, lambda *_: (0,) * len(shape), pipeline_mode=pl.Buffered(1))


def _rms(xf, gain):
    return xf * lax.rsqrt(jnp.mean(xf * xf, axis=-1, keepdims=True) + EPS) * gain


def _dot(a, b):
    return jnp.dot(a, b, preferred_element_type=F32)


def _ffn_kernel(x_ref, gain_ref, wg_ref, wu_ref, wd_ref, *refs, round_weights, aliased, n_extra):
    refs = refs[1:] if aliased else refs
    extra_in, refs = refs[:n_extra], refs[n_extra:]
    o_ref, xn_ref = refs[0], refs[-1]
    if round_weights:
        for src, dst in zip((wg_ref, wu_ref, wd_ref), refs[1:4]):
            dst[...] = src[...].astype(dst.dtype)
        wg_ref, wu_ref, wd_ref = refs[1:4]
    for src, dst in zip(extra_in, refs[-1 - n_extra:-1]):
        dst[...] = src[...].astype(dst.dtype)

    first = pl.program_id(1) == 0

    @pl.when(first)
    def _():
        xn_ref[...] = _rms(x_ref[...], gain_ref[...]).astype(BF16)

    def accumulate(base_ref):
        xn = xn_ref[...]
        g = _dot(xn, wg_ref[...])
        u = _dot(xn, wu_ref[...])
        h = (g * jax.nn.sigmoid(g) * u * 0.5).astype(BF16)
        o_ref[...] = base_ref[...] + _dot(h, wd_ref[...])

    pl.when(first)(lambda: accumulate(x_ref))
    pl.when(jnp.logical_not(first))(lambda: accumulate(o_ref))


def _ffn_specs(d, tm, tf, first_tile):
    row_block = pl.BlockSpec((tm, d), lambda i, j: (i + first_tile, 0))
    weight_specs = [
        pl.BlockSpec((d, tf), lambda i, j: (0, j)),
        pl.BlockSpec((d, tf), lambda i, j: (0, j)),
        pl.BlockSpec((tf, d), lambda i, j: (j, 0)),
    ]
    return row_block, [row_block, pl.BlockSpec((1, d), lambda i, j: (0, 0))] + weight_specs, weight_specs


def _ffn_head(x, gain, wg32, wu32, wd32):
    s, d = x.shape
    f = wg32.shape[1]
    tm, tf = FFN_ROWS, FFN_HEAD_COLS
    row_block, in_specs, weight_specs = _ffn_specs(d, tm, tf, 0)
    outs = pl.pallas_call(
        functools.partial(_ffn_kernel, round_weights=True, aliased=False, n_extra=0),
        name="ffn_head",
        grid=(FFN_HEAD_TILES, f // tf),
        in_specs=in_specs,
        out_specs=[row_block] + weight_specs,
        out_shape=[jax.ShapeDtypeStruct((s, d), F32)]
        + [jax.ShapeDtypeStruct(w.shape, BF16) for w in (wg32, wu32, wd32)],
        scratch_shapes=[pltpu.VMEM((tm, d), BF16)],
        compiler_params=_compiler_params(("arbitrary", "arbitrary")),
    )(x, gain, wg32, wu32, wd32)
    return outs[0], tuple(outs[1:])


def _flat_step_row_block(shape, grid):
    rows, cols = shape
    n_blocks = rows // BF16_SUBLANES
    assert rows % BF16_SUBLANES == 0 and n_blocks <= grid[0] * grid[1]
    return pl.BlockSpec((BF16_SUBLANES, cols),
                        lambda i, j: (jnp.minimum(i * grid[1] + j, n_blocks - 1), 0))


def _ffn(x, gain, wg, wu, wd, into=None, round_weights=()):
    s, d = x.shape
    f = wg.shape[1]
    tm, tf = FFN_ROWS, FFN_COLS
    first_tile = 0 if into is None else FFN_HEAD_TILES
    grid = (s // tm - first_tile, f // tf)
    row_block, in_specs, _ = _ffn_specs(d, tm, tf, first_tile)
    operands = [x, gain, wg, wu, wd]
    aliases = {}
    if into is not None:
        in_specs.append(pl.BlockSpec(memory_space=pl.ANY))
        aliases = {len(operands): 0}
        operands.append(into)
    extra_specs = [_flat_step_row_block(w.shape, grid) for w in round_weights]
    outs = pl.pallas_call(
        functools.partial(_ffn_kernel, round_weights=False, aliased=into is not None,
                          n_extra=len(round_weights)),
        name="ffn",
        grid=grid,
        in_specs=in_specs + extra_specs,
        out_specs=[row_block] + extra_specs,
        out_shape=[jax.ShapeDtypeStruct((s, d), F32)]
        + [jax.ShapeDtypeStruct(w.shape, BF16) for w in round_weights],
        input_output_aliases=aliases,
        scratch_shapes=[pltpu.VMEM((tm, d), BF16)],
        compiler_params=_compiler_params(
            ("arbitrary" if round_weights else "parallel", "arbitrary")),
    )(*operands, *round_weights)
    return outs[0], tuple(outs[1:])


def _head_norm_store(acc, gain, scale, out_ref):
    for h in range(N_HEADS):
        sl = slice(h * HEAD_DIM, (h + 1) * HEAD_DIM)
        out_ref[:, sl] = (_rms(acc[:, sl], gain) * scale).astype(out_ref.dtype)


def _build_attention_bias(i, table_ref, bias_ref):
    tq, tk = MIX_ROWS, MIX_KEYS
    r = lax.broadcasted_iota(jnp.int32, (tq, tk), 0)
    c = lax.broadcasted_iota(jnp.int32, (tq, tk), 1)
    q_chunk = r // CHUNK
    k_chunk = c // CHUNK
    first_key = (MIX_KEY_BLOCKS - 1 - i) * tq
    visible = (k_chunk >= q_chunk) & (k_chunk <= q_chunk + LEFT_CHUNKS) & (c >= first_key)
    for h in range(N_HEADS):
        rows = jnp.broadcast_to(table_ref[h:h + 1, :], (tq, BIAS_TABLE))
        rolled = pltpu.roll(rows, 0, 1, stride=1, stride_axis=0)
        bias_ref[h] = jnp.where(visible, rolled[:, :tk] * LOG2E, MASK_VALUE)


def _window_sums(z, w):
    span = 1
    while span < w:
        z = z + pltpu.roll(z, span, 0)
        span *= 2
    return z


def _mixer_kernel(h_ref, gain_ref, w_ref, qg_ref, kg_ref, table_ref, pw_ref, ps_ref, *refs):
    n_cast = (len(refs) - 8) // 2
    cast_in, (yp_ref, ya_ref, u_ref), cast_out = (refs[:n_cast], refs[n_cast:n_cast + 3],
                                                  refs[n_cast + 3:-5])
    q_ref, k_ring, v_ring, zs_ref, bias_ref = refs[-5:]
    nb = MIX_KEY_BLOCKS
    tq = MIX_ROWS
    width = ATTN_WIDTH
    i = pl.program_id(0)

    @pl.when(i < nb)
    def _():
        _build_attention_bias(i, table_ref, bias_ref)

    @pl.when(i == 0)
    def _():
        k_ring[...] = jnp.zeros_like(k_ring)
        v_ring[...] = jnp.zeros_like(v_ring)
        zs_ref[:POOL_HALO, :] = jnp.zeros((POOL_HALO, width), F32)

    @pl.when(i > 0)
    def _():
        zs_ref[:POOL_HALO, :] = zs_ref[tq:, :]

    def round_weights(part, parts=4):
        for src, dst in list(zip(cast_in, cast_out))[part::parts]:
            dst[...] = src[...].astype(dst.dtype)

    slot = lax.rem(i, nb)
    u_ref[...] = _rms(h_ref[...], gain_ref[...]).astype(BF16)
    zs_ref[POOL_HALO:, :] = _dot(u_ref[...], w_ref[:, 0:width])
    round_weights(0)
    _head_norm_store(_dot(u_ref[...], w_ref[:, width:2 * width]),
                     qg_ref[...], HEAD_DIM ** -0.5 * LOG2E, q_ref)
    round_weights(1)
    _head_norm_store(_dot(u_ref[...], w_ref[:, 2 * width:3 * width]),
                     kg_ref[...], 1.0, k_ring.at[slot])
    round_weights(2)
    v_ring[slot] = _dot(u_ref[...], w_ref[:, 3 * width:4 * width]).astype(BF16)
    round_weights(3)

    t = i * tq + lax.broadcasted_iota(jnp.int32, (tq, 1), 0)
    for g, w in enumerate(POOL_WINDOWS):
        cols = slice(g * POOL_GROUP_DIM, (g + 1) * POOL_GROUP_DIM)
        z = zs_ref[:, cols]
        inv_count = 1.0 / jnp.minimum(t + 1, w).astype(F32)
        diff = (_window_sums(z, w)[POOL_HALO:] * inv_count - z[POOL_HALO:]).astype(BF16)
        y = _dot(diff, pw_ref[g]) * ps_ref[:, cols]
        yp_ref[:, cols] = y.astype(yp_ref.dtype)

    slots = [lax.rem(i + 1 + b, nb) for b in range(nb)]

    def head_scores(h):
        hs = slice(h * HEAD_DIM, (h + 1) * HEAD_DIM)
        qh = q_ref[:, hs]
        return [lax.dot_general(qh, k_ring[slots[b], :, hs], (((1,), (1,)), ((), ())),
                                preferred_element_type=F32) + bias_ref[h, :, b * tq:(b + 1) * tq]
                for b in range(nb)]

    def head_probs(scores):
        m = scores[0].max(axis=-1, keepdims=True)
        for b in range(1, nb):
            m = jnp.maximum(m, scores[b].max(axis=-1, keepdims=True))
        probs = [jnp.exp2(sc - m) for sc in scores]
        denom = probs[0].sum(axis=-1, keepdims=True)
        for pr in probs[1:]:
            denom = denom + pr.sum(axis=-1, keepdims=True)
        return [pr.astype(BF16) for pr in probs], denom

    def head_output(h, probs, denom):
        hs = slice(h * HEAD_DIM, (h + 1) * HEAD_DIM)
        out = _dot(probs[0], v_ring[slots[0], :, hs])
        for b in range(1, nb):
            out = out + _dot(probs[b], v_ring[slots[b], :, hs])
        ya_ref[:, hs] = (out / denom).astype(ya_ref.dtype)

    ahead = MIX_HEADS_AHEAD
    pending = {h: head_scores(h) for h in range(ahead)}
    for h in range(N_HEADS):
        if h + ahead < N_HEADS:
            pending[h + ahead] = head_scores(h + ahead)
        head_output(h, *head_probs(pending.pop(h)))


def _bias_table(rel_bias):
    far = rel_bias[:, 2 * REL_CLIP:]
    head = jnp.broadcast_to(far, (rel_bias.shape[0], LEFT - REL_CLIP))
    body = rel_bias[:, :0:-1]
    tail = jnp.broadcast_to(far, (rel_bias.shape[0],
                                  BIAS_TABLE - (LEFT - REL_CLIP) - 2 * REL_CLIP))
    return jnp.concatenate([head, body, tail], axis=1).astype(F32)


def _row_block_per_step(shape, steps):
    rows, cols = shape
    rep = 1
    while rows % (steps // rep) or (rows // (steps // rep)) % BF16_SUBLANES:
        rep *= 2
    return pl.BlockSpec((rows // (steps // rep), cols), lambda i: (i // rep, 0))


def _mixer(h, gain, w_in, q_gain, k_gain, table, pool_w, pool_scale, round_weights=()):
    s, d = h.shape
    width = ATTN_WIDTH
    tq, nb = MIX_ROWS, MIX_KEY_BLOCKS
    steps = s // tq
    row_block = pl.BlockSpec((tq, width), lambda i: (i, 0))
    cast_specs = [_row_block_per_step(w.shape, steps) for w in round_weights]
    outs = pl.pallas_call(
        _mixer_kernel,
        name="mixer",
        grid=(steps,),
        in_specs=[
            pl.BlockSpec((tq, d), lambda i: (i, 0)),
            _resident((1, d)),
            _resident(w_in.shape),
            _resident((1, HEAD_DIM)),
            _resident((1, HEAD_DIM)),
            _resident(table.shape),
            _resident(pool_w.shape),
            _resident((1, width)),
        ] + cast_specs,
        out_specs=[row_block, row_block, pl.BlockSpec((tq, d), lambda i: (i, 0))] + cast_specs,
        out_shape=[
            jax.ShapeDtypeStruct((s, width), BF16),
            jax.ShapeDtypeStruct((s, width), BF16),
            jax.ShapeDtypeStruct((s, d), BF16),
        ] + [jax.ShapeDtypeStruct(w.shape, BF16) for w in round_weights],
        scratch_shapes=[
            pltpu.VMEM((tq, width), BF16),
            pltpu.VMEM((nb, tq, width), BF16),
            pltpu.VMEM((nb, tq, width), BF16),
            pltpu.VMEM((POOL_HALO + tq, width), F32),
            pltpu.VMEM((N_HEADS, tq, MIX_KEYS), F32),
        ],
        compiler_params=_compiler_params(("arbitrary",)),
    )(h, gain, w_in, q_gain, k_gain, table, pool_w, pool_scale, *round_weights)
    return outs[0], outs[1], outs[2], tuple(outs[3:])


def _merge_kernel(h_ref, u_ref, yp_ref, ya_ref, wg_ref, bg_ref, wa_ref, wb_ref, wo_ref, o_ref):
    d = h_ref.shape[1]
    g_pool = jax.nn.sigmoid(_dot(u_ref[...], wg_ref[:, :d]) + bg_ref[:, :d])
    merged = g_pool * _dot(yp_ref[...], wa_ref[...])
    g_attn = jax.nn.sigmoid(_dot(u_ref[...], wg_ref[:, d:]) + bg_ref[:, d:])
    merged = merged + g_attn * _dot(ya_ref[...], wb_ref[...])
    o_ref[...] = h_ref[...] + _dot(merged.astype(BF16), wo_ref[...])


def _merge(h, u, y_pool, y_attn, w_g, b_g, w_a, w_b, w_out):
    s, d = h.shape
    width = y_pool.shape[1]
    tm = MERGE_ROWS
    return pl.pallas_call(
        _merge_kernel,
        name="merge",
        grid=(s // tm,),
        in_specs=[
            pl.BlockSpec((tm, d), lambda i: (i, 0)),
            pl.BlockSpec((tm, d), lambda i: (i, 0)),
            pl.BlockSpec((tm, width), lambda i: (i, 0)),
            pl.BlockSpec((tm, width), lambda i: (i, 0)),
            _resident(w_g.shape),
            _resident(b_g.shape),
            _resident(w_a.shape),
            _resident(w_b.shape),
            _resident(w_out.shape),
        ],
        out_specs=pl.BlockSpec((tm, d), lambda i: (i, 0)),
        out_shape=jax.ShapeDtypeStruct((s, d), F32),
        compiler_params=_compiler_params(("parallel",)),
    )(h, u, y_pool, y_attn, w_g, b_g, w_a, w_b, w_out)


def _ple_kernel(h_ref, p_ref, gain_ref, wpg_ref, wple_ref, o_ref):
    t = _rms(h_ref[...], gain_ref[...]).astype(BF16)
    gate = jax.nn.sigmoid(_dot(t, wpg_ref[...]))
    emb = _dot(p_ref[...].astype(BF16), wple_ref[...])
    o_ref[...] = h_ref[...] + gate * emb


def _ple(h, p, gain, w_pg, w_ple):
    s, d = h.shape
    tm = PLE_ROWS
    return pl.pallas_call(
        _ple_kernel,
        name="ple",
        grid=(s // tm,),
        in_specs=[
            pl.BlockSpec((tm, d), lambda i: (i, 0)),
            pl.BlockSpec((tm, p.shape[1]), lambda i: (i, 0)),
            _resident((1, d)),
            _resident(w_pg.shape),
            _resident(w_ple.shape),
        ],
        out_specs=pl.BlockSpec((tm, d), lambda i: (i, 0)),
        out_shape=jax.ShapeDtypeStruct((s, d), F32),
        compiler_params=_compiler_params(("parallel",)),
    )(h, p, gain, w_pg, w_ple)


def kernel(x, p, ffn1_norm, ffn1_w_gate, ffn1_w_up, ffn1_w_down, mix_norm, w_in, pool_w, pool_scale, q_norm, k_norm, rel_bias, w_br_pool, w_br_attn, w_branch_gate, b_branch_gate, w_out, ffn2_norm, ffn2_w_gate, ffn2_w_up, ffn2_w_down, ple_norm, w_ple_gate, w_ple):
    batch, seq, d_model = x.shape
    assert batch == 1, "the row tiling treats the sequence as the only row axis"
    depth = p.shape[0]
    h = x.reshape(seq, d_model)
    for i in range(depth):
        row = lambda a: a[i].reshape(1, -1)
        mm = lambda a: a[i].astype(BF16)
        h_head, ffn1_weights = _ffn_head(h, row(ffn1_norm), ffn1_w_gate[i], ffn1_w_up[i],
                                         ffn1_w_down[i])
        h, (w_in16,) = _ffn(h, row(ffn1_norm), *ffn1_weights, into=h_head,
                            round_weights=(w_in[i],))
        later = (w_branch_gate[i], w_br_pool[i], w_br_attn[i], w_out[i], w_ple_gate[i],
                 ffn2_w_gate[i], ffn2_w_up[i], ffn2_w_down[i])
        y_pool, y_attn, u, later = _mixer(h, row(mix_norm), w_in16, row(q_norm), row(k_norm),
                                          _bias_table(rel_bias[i]), mm(pool_w), row(pool_scale),
                                          round_weights=later)
        w_g, w_a, w_b, w_o, w_pg = later[:5]
        h = _merge(h, u, y_pool, y_attn, w_g, row(b_branch_gate), w_a, w_b, w_o)
        h, _ = _ffn(h, row(ffn2_norm), *later[5:])
        h = _ple(h, p[i].reshape(seq, -1), row(ple_norm), w_pg, mm(w_ple))
    return h.reshape(batch, seq, d_model)
```

```python
import functools
import math

import jax
import jax.numpy as jnp
from jax import lax
from jax.experimental import pallas as pl
from jax.experimental.pallas import tpu as pltpu

F32 = jnp.float32
BF16 = jnp.bfloat16

EPS = 1e-6
MASK_VALUE = -1e30
LOG2E = math.log2(math.e)
CHUNK = 64
LEFT_CHUNKS = 8
LEFT = LEFT_CHUNKS * CHUNK
REL_CLIP = 256
N_HEADS = 8
HEAD_DIM = 128
ATTN_WIDTH = N_HEADS * HEAD_DIM
POOL_WINDOWS = (2, 4, 8, 16)
POOL_GROUP_DIM = 256
POOL_HALO = 16

V7X_VMEM_BYTES = 64 * 1024 * 1024
VMEM_LIMIT_BYTES = V7X_VMEM_BYTES - 6 * 1024 * 1024

FFN_ROWS = 1024
FFN_COLS = 512
FFN_HEAD_TILES = 1
FFN_HEAD_COLS = 256
BF16_SUBLANES = 16
MIX_ROWS = 256
MIX_KEY_BLOCKS = 1 + LEFT // MIX_ROWS
MIX_KEYS = MIX_KEY_BLOCKS * MIX_ROWS
MIX_HEADS_AHEAD = 1
BIAS_TABLE = 1024
MERGE_ROWS = 256
PLE_ROWS = 1024


def _compiler_params(semantics):
    return pltpu.CompilerParams(dimension_semantics=semantics,
                                vmem_limit_bytes=VMEM_LIMIT_BYTES)


def _resident(shape):
    return pl.BlockSpec(shape, lambda *_: (0,) * len(shape), pipeline_mode=pl.Buffered(1))


def _rms(xf, gain):
    return xf * lax.rsqrt(jnp.mean(xf * xf, axis=-1, keepdims=True) + EPS) * gain


def _dot(a, b):
    return jnp.dot(a, b, preferred_element_type=F32)


def _ffn_kernel(x_ref, gain_ref, wg_ref, wu_ref, wd_ref, *refs, round_weights, aliased, n_extra):
    refs = refs[1:] if aliased else refs
    extra_in, refs = refs[:n_extra], refs[n_extra:]
    o_ref, xn_ref = refs[0], refs[-1]
    rounding = list(zip(extra_in, refs[-1 - n_extra:-1]))
    if round_weights:
        rounding += list(zip((wg_ref, wu_ref, wd_ref), refs[1:4]))
        wg_ref, wu_ref, wd_ref = refs[1:4]

    def step(base_ref, normalize):
        for src, dst in rounding:
            dst[...] = src[...].astype(dst.dtype)
        if normalize:
            xn_ref[...] = _rms(x_ref[...], gain_ref[...]).astype(BF16)
        xn = xn_ref[...]
        half = wg_ref.shape[1] // 2
        gu = [(_dot(xn, wg_ref[:, c:c + half]), _dot(xn, wu_ref[:, c:c + half]))
              for c in (0, half)]
        acc = base_ref[...]
        for (g, u), c in zip(gu, (0, half)):
            h = (g * jax.nn.sigmoid(g) * u * 0.5).astype(BF16)
            acc = acc + _dot(h, wd_ref[c:c + half, :])
        o_ref[...] = acc

    first = pl.program_id(1) == 0
    pl.when(first)(lambda: step(x_ref, True))
    pl.when(jnp.logical_not(first))(lambda: step(o_ref, False))


def _ffn_specs(d, tm, tf, first_tile):
    row_block = pl.BlockSpec((tm, d), lambda i, j: (i + first_tile, 0))
    weight_specs = [
        pl.BlockSpec((d, tf), lambda i, j: (0, j)),
        pl.BlockSpec((d, tf), lambda i, j: (0, j)),
        pl.BlockSpec((tf, d), lambda i, j: (j, 0)),
    ]
    return row_block, [row_block, pl.BlockSpec((1, d), lambda i, j: (0, 0))] + weight_specs, weight_specs


def _ffn_head(x, gain, wg32, wu32, wd32):
    s, d = x.shape
    f = wg32.shape[1]
    tm, tf = FFN_ROWS, FFN_HEAD_COLS
    row_block, in_specs, weight_specs = _ffn_specs(d, tm, tf, 0)
    outs = pl.pallas_call(
        functools.partial(_ffn_kernel, round_weights=True, aliased=False, n_extra=0),
        name="ffn_head",
        grid=(FFN_HEAD_TILES, f // tf),
        in_specs=in_specs,
        out_specs=[row_block] + weight_specs,
        out_shape=[jax.ShapeDtypeStruct((s, d), F32)]
        + [jax.ShapeDtypeStruct(w.shape, BF16) for w in (wg32, wu32, wd32)],
        scratch_shapes=[pltpu.VMEM((tm, d), BF16)],
        compiler_params=_compiler_params(("arbitrary", "arbitrary")),
    )(x, gain, wg32, wu32, wd32)
    return outs[0], tuple(outs[1:])


def _flat_step_row_block(shape, grid):
    rows, cols = shape
    n_blocks = rows // BF16_SUBLANES
    assert rows % BF16_SUBLANES == 0 and n_blocks <= grid[0] * grid[1]
    return pl.BlockSpec((BF16_SUBLANES, cols),
                        lambda i, j: (jnp.minimum(i * grid[1] + j, n_blocks - 1), 0))


def _ffn(x, gain, wg, wu, wd, into=None, round_weights=()):
    s, d = x.shape
    f = wg.shape[1]
    tm, tf = FFN_ROWS, FFN_COLS
    first_tile = 0 if into is None else FFN_HEAD_TILES
    grid = (s // tm - first_tile, f // tf)
    row_block, in_specs, _ = _ffn_specs(d, tm, tf, first_tile)
    operands = [x, gain, wg, wu, wd]
    aliases = {}
    if into is not None:
        in_specs.append(pl.BlockSpec(memory_space=pl.ANY))
        aliases = {len(operands): 0}
        operands.append(into)
    extra_specs = [_flat_step_row_block(w.shape, grid) for w in round_weights]
    outs = pl.pallas_call(
        functools.partial(_ffn_kernel, round_weights=False, aliased=into is not None,
                          n_extra=len(round_weights)),
        name="ffn",
        grid=grid,
        in_specs=in_specs + extra_specs,
        out_specs=[row_block] + extra_specs,
        out_shape=[jax.ShapeDtypeStruct((s, d), F32)]
        + [jax.ShapeDtypeStruct(w.shape, BF16) for w in round_weights],
        input_output_aliases=aliases,
        scratch_shapes=[pltpu.VMEM((tm, d), BF16)],
        compiler_params=_compiler_params(
            ("arbitrary" if round_weights else "parallel", "arbitrary")),
    )(*operands, *round_weights)
    return outs[0], tuple(outs[1:])


def _head_norm_store(acc, gain, scale, out_ref):
    for h in range(N_HEADS):
        sl = slice(h * HEAD_DIM, (h + 1) * HEAD_DIM)
        out_ref[:, sl] = (_rms(acc[:, sl], gain) * scale).astype(out_ref.dtype)


def _build_attention_bias(i, table_ref, bias_ref):
    tq, tk = MIX_ROWS, MIX_KEYS
    r = lax.broadcasted_iota(jnp.int32, (tq, tk), 0)
    c = lax.broadcasted_iota(jnp.int32, (tq, tk), 1)
    q_chunk = r // CHUNK
    k_chunk = c // CHUNK
    first_key = (MIX_KEY_BLOCKS - 1 - i) * tq
    visible = (k_chunk >= q_chunk) & (k_chunk <= q_chunk + LEFT_CHUNKS) & (c >= first_key)
    for h in range(N_HEADS):
        rows = jnp.broadcast_to(table_ref[h:h + 1, :], (tq, BIAS_TABLE))
        rolled = pltpu.roll(rows, 0, 1, stride=1, stride_axis=0)
        bias_ref[h] = jnp.where(visible, rolled[:, :tk] * LOG2E, MASK_VALUE)


def _window_sums(z, w):
    span = 1
    while span < w:
        z = z + pltpu.roll(z, span, 0)
        span *= 2
    return z


def _mixer_kernel(h_ref, gain_ref, w_ref, qg_ref, kg_ref, table_ref, pw_ref, ps_ref, *refs):
    n_cast = (len(refs) - 8) // 2
    cast_in, (yp_ref, ya_ref, u_ref), cast_out = (refs[:n_cast], refs[n_cast:n_cast + 3],
                                                  refs[n_cast + 3:-5])
    q_ref, k_ring, v_ring, zs_ref, bias_ref = refs[-5:]
    nb = MIX_KEY_BLOCKS
    tq = MIX_ROWS
    width = ATTN_WIDTH
    i = pl.program_id(0)

    @pl.when(i < nb)
    def _():
        _build_attention_bias(i, table_ref, bias_ref)

    @pl.when(i == 0)
    def _():
        k_ring[...] = jnp.zeros_like(k_ring)
        v_ring[...] = jnp.zeros_like(v_ring)
        zs_ref[:POOL_HALO, :] = jnp.zeros((POOL_HALO, width), F32)

    @pl.when(i > 0)
    def _():
        zs_ref[:POOL_HALO, :] = zs_ref[tq:, :]

    def round_weights(part, parts=4):
        for src, dst in list(zip(cast_in, cast_out))[part::parts]:
            dst[...] = src[...].astype(dst.dtype)

    slot = lax.rem(i, nb)
    u_ref[...] = _rms(h_ref[...], gain_ref[...]).astype(BF16)
    zs_ref[POOL_HALO:, :] = _dot(u_ref[...], w_ref[:, 0:width])
    round_weights(0)
    _head_norm_store(_dot(u_ref[...], w_ref[:, width:2 * width]),
                     qg_ref[...], HEAD_DIM ** -0.5 * LOG2E, q_ref)
    round_weights(1)
    _head_norm_store(_dot(u_ref[...], w_ref[:, 2 * width:3 * width]),
                     kg_ref[...], 1.0, k_ring.at[slot])
    round_weights(2)
    v_ring[slot] = _dot(u_ref[...], w_ref[:, 3 * width:4 * width]).astype(BF16)
    round_weights(3)

    t = i * tq + lax.broadcasted_iota(jnp.int32, (tq, 1), 0)
    for g, w in enumerate(POOL_WINDOWS):
        cols = slice(g * POOL_GROUP_DIM, (g + 1) * POOL_GROUP_DIM)
        z = zs_ref[:, cols]
        inv_count = 1.0 / jnp.minimum(t + 1, w).astype(F32)
        diff = (_window_sums(z, w)[POOL_HALO:] * inv_count - z[POOL_HALO:]).astype(BF16)
        y = _dot(diff, pw_ref[g]) * ps_ref[:, cols]
        yp_ref[:, cols] = y.astype(yp_ref.dtype)

    slots = [lax.rem(i + 1 + b, nb) for b in range(nb)]

    def head_scores(h):
        hs = slice(h * HEAD_DIM, (h + 1) * HEAD_DIM)
        qh = q_ref[:, hs]
        return [lax.dot_general(qh, k_ring[slots[b], :, hs], (((1,), (1,)), ((), ())),
                                preferred_element_type=F32) + bias_ref[h, :, b * tq:(b + 1) * tq]
                for b in range(nb)]

    def head_probs(scores):
        m = scores[0].max(axis=-1, keepdims=True)
        for b in range(1, nb):
            m = jnp.maximum(m, scores[b].max(axis=-1, keepdims=True))
        probs = [jnp.exp2(sc - m) for sc in scores]
        denom = probs[0].sum(axis=-1, keepdims=True)
        for pr in probs[1:]:
            denom = denom + pr.sum(axis=-1, keepdims=True)
        return [pr.astype(BF16) for pr in probs], denom

    def head_output(h, probs, denom):
        hs = slice(h * HEAD_DIM, (h + 1) * HEAD_DIM)
        out = _dot(probs[0], v_ring[slots[0], :, hs])
        for b in range(1, nb):
            out = out + _dot(probs[b], v_ring[slots[b], :, hs])
        ya_ref[:, hs] = (out / denom).astype(ya_ref.dtype)

    ahead = MIX_HEADS_AHEAD
    pending = {h: head_scores(h) for h in range(ahead)}
    for h in range(N_HEADS):
        if h + ahead < N_HEADS:
            pending[h + ahead] = head_scores(h + ahead)
        head_output(h, *head_probs(pending.pop(h)))


def _bias_table(rel_bias):
    far = rel_bias[:, 2 * REL_CLIP:]
    head = jnp.broadcast_to(far, (rel_bias.shape[0], LEFT - REL_CLIP))
    body = rel_bias[:, :0:-1]
    tail = jnp.broadcast_to(far, (rel_bias.shape[0],
                                  BIAS_TABLE - (LEFT - REL_CLIP) - 2 * REL_CLIP))
    return jnp.concatenate([head, body, tail], axis=1).astype(F32)


def _row_block_per_step(shape, steps):
    rows, cols = shape
    rep = 1
    while rows % (steps // rep) or (rows // (steps // rep)) % BF16_SUBLANES:
        rep *= 2
    return pl.BlockSpec((rows // (steps // rep), cols), lambda i: (i // rep, 0))


def _mixer(h, gain, w_in, q_gain, k_gain, table, pool_w, pool_scale, round_weights=()):
    s, d = h.shape
    width = ATTN_WIDTH
    tq, nb = MIX_ROWS, MIX_KEY_BLOCKS
    steps = s // tq
    row_block = pl.BlockSpec((tq, width), lambda i: (i, 0))
    cast_specs = [_row_block_per_step(w.shape, steps) for w in round_weights]
    outs = pl.pallas_call(
        _mixer_kernel,
        name="mixer",
        grid=(steps,),
        in_specs=[
            pl.BlockSpec((tq, d), lambda i: (i, 0)),
            _resident((1, d)),
            _resident(w_in.shape),
            _resident((1, HEAD_DIM)),
            _resident((1, HEAD_DIM)),
            _resident(table.shape),
            _resident(pool_w.shape),
            _resident((1, width)),
        ] + cast_specs,
        out_specs=[row_block, row_block, pl.BlockSpec((tq, d), lambda i: (i, 0))] + cast_specs,
        out_shape=[
            jax.ShapeDtypeStruct((s, width), BF16),
            jax.ShapeDtypeStruct((s, width), BF16),
            jax.ShapeDtypeStruct((s, d), BF16),
        ] + [jax.ShapeDtypeStruct(w.shape, BF16) for w in round_weights],
        scratch_shapes=[
            pltpu.VMEM((tq, width), BF16),
            pltpu.VMEM((nb, tq, width), BF16),
            pltpu.VMEM((nb, tq, width), BF16),
            pltpu.VMEM((POOL_HALO + tq, width), F32),
            pltpu.VMEM((N_HEADS, tq, MIX_KEYS), F32),
        ],
        compiler_params=_compiler_params(("arbitrary",)),
    )(h, gain, w_in, q_gain, k_gain, table, pool_w, pool_scale, *round_weights)
    return outs[0], outs[1], outs[2], tuple(outs[3:])


def _merge_kernel(h_ref, u_ref, yp_ref, ya_ref, wg_ref, bg_ref, wa_ref, wb_ref, wo_ref, o_ref):
    d = h_ref.shape[1]
    g_pool = jax.nn.sigmoid(_dot(u_ref[...], wg_ref[:, :d]) + bg_ref[:, :d])
    merged = g_pool * _dot(yp_ref[...], wa_ref[...])
    g_attn = jax.nn.sigmoid(_dot(u_ref[...], wg_ref[:, d:]) + bg_ref[:, d:])
    merged = merged + g_attn * _dot(ya_ref[...], wb_ref[...])
    o_ref[...] = h_ref[...] + _dot(merged.astype(BF16), wo_ref[...])


def _merge(h, u, y_pool, y_attn, w_g, b_g, w_a, w_b, w_out):
    s, d = h.shape
    width = y_pool.shape[1]
    tm = MERGE_ROWS
    return pl.pallas_call(
        _merge_kernel,
        name="merge",
        grid=(s // tm,),
        in_specs=[
            pl.BlockSpec((tm, d), lambda i: (i, 0)),
            pl.BlockSpec((tm, d), lambda i: (i, 0)),
            pl.BlockSpec((tm, width), lambda i: (i, 0)),
            pl.BlockSpec((tm, width), lambda i: (i, 0)),
            _resident(w_g.shape),
            _resident(b_g.shape),
            _resident(w_a.shape),
            _resident(w_b.shape),
            _resident(w_out.shape),
        ],
        out_specs=pl.BlockSpec((tm, d), lambda i: (i, 0)),
        out_shape=jax.ShapeDtypeStruct((s, d), F32),
        compiler_params=_compiler_params(("parallel",)),
    )(h, u, y_pool, y_attn, w_g, b_g, w_a, w_b, w_out)


def _ple_kernel(h_ref, p_ref, gain_ref, wpg_ref, wple_ref, o_ref):
    t = _rms(h_ref[...], gain_ref[...]).astype(BF16)
    gate = jax.nn.sigmoid(_dot(t, wpg_ref[...]))
    emb = _dot(p_ref[...].astype(BF16), wple_ref[...])
    o_ref[...] = h_ref[...] + gate * emb


def _ple(h, p, gain, w_pg, w_ple):
    s, d = h.shape
    tm = PLE_ROWS
    return pl.pallas_call(
        _ple_kernel,
        name="ple",
        grid=(s // tm,),
        in_specs=[
            pl.BlockSpec((tm, d), lambda i: (i, 0)),
            pl.BlockSpec((tm, p.shape[1]), lambda i: (i, 0)),
            _resident((1, d)),
            _resident(w_pg.shape),
            _resident(w_ple.shape),
        ],
        out_specs=pl.BlockSpec((tm, d), lambda i: (i, 0)),
        out_shape=jax.ShapeDtypeStruct((s, d), F32),
        compiler_params=_compiler_params(("parallel",)),
    )(h, p, gain, w_pg, w_ple)


def kernel(x, p, ffn1_norm, ffn1_w_gate, ffn1_w_up, ffn1_w_down, mix_norm, w_in, pool_w, pool_scale, q_norm, k_norm, rel_bias, w_br_pool, w_br_attn, w_branch_gate, b_branch_gate, w_out, ffn2_norm, ffn2_w_gate, ffn2_w_up, ffn2_w_down, ple_norm, w_ple_gate, w_ple):
    batch, seq, d_model = x.shape
    assert batch == 1, "the row tiling treats the sequence as the only row axis"
    depth = p.shape[0]
    h = x.reshape(seq, d_model)
    for i in range(depth):
        row = lambda a: a[i].reshape(1, -1)
        mm = lambda a: a[i].astype(BF16)
        h_head, ffn1_weights = _ffn_head(h, row(ffn1_norm), ffn1_w_gate[i], ffn1_w_up[i],
                                         ffn1_w_down[i])
        h, (w_in16,) = _ffn(h, row(ffn1_norm), *ffn1_weights, into=h_head,
                            round_weights=(w_in[i],))
        later = (w_branch_gate[i], w_br_pool[i], w_br_attn[i], w_out[i], w_ple_gate[i],
                 ffn2_w_gate[i], ffn2_w_up[i], ffn2_w_down[i])
        y_pool, y_attn, u, later = _mixer(h, row(mix_norm), w_in16, row(q_norm), row(k_norm),
                                          _bias_table(rel_bias[i]), mm(pool_w), row(pool_scale),
                                          round_weights=later)
        w_g, w_a, w_b, w_o, w_pg = later[:5]
        h = _merge(h, u, y_pool, y_attn, w_g, row(b_branch_gate), w_a, w_b, w_o)
        h, _ = _ffn(h, row(ffn2_norm), *later[5:])
        h = _ple(h, p[i].reshape(seq, -1), row(ple_norm), w_pg, mm(w_ple))
    return h.reshape(batch, seq, d_model)
```

```python
import functools
import math

import jax
import jax.numpy as jnp
from jax import lax
from jax.experimental import pallas as pl
from jax.experimental.pallas import tpu as pltpu

F32 = jnp.float32
BF16 = jnp.bfloat16

EPS = 1e-6
MASK_VALUE = -1e30
LOG2E = math.log2(math.e)
CHUNK = 64
LEFT_CHUNKS = 8
LEFT = LEFT_CHUNKS * CHUNK
REL_CLIP = 256
N_HEADS = 8
HEAD_DIM = 128
ATTN_WIDTH = N_HEADS * HEAD_DIM
POOL_WINDOWS = (2, 4, 8, 16)
POOL_GROUP_DIM = 256
POOL_HALO = 16

V7X_VMEM_BYTES = 64 * 1024 * 1024
VMEM_LIMIT_BYTES = V7X_VMEM_BYTES - 6 * 1024 * 1024

FFN_ROWS = 1024
FFN_COLS = 512
FFN_HEAD_TILES = 1
FFN_HEAD_COLS = 256
BF16_SUBLANES = 16
V7X_MXU_COLUMNS = 256
MIX_ROWS = 256
MIX_KEY_BLOCKS = 1 + LEFT // MIX_ROWS
MIX_KEYS = MIX_KEY_BLOCKS * MIX_ROWS
MIX_HEADS_AHEAD = 1
BIAS_TABLE = 1024
MERGE_ROWS = 256
PLE_ROWS = 1024


def _compiler_params(semantics):
    return pltpu.CompilerParams(dimension_semantics=semantics,
                                vmem_limit_bytes=VMEM_LIMIT_BYTES)


def _resident(shape):
    return pl.BlockSpec(shape, lambda *_: (0,) * len(shape), pipeline_mode=pl.Buffered(1))


def _rms(xf, gain):
    return xf * lax.rsqrt(jnp.mean(xf * xf, axis=-1, keepdims=True) + EPS) * gain


def _dot(a, b):
    return jnp.dot(a, b, preferred_element_type=F32)


def _ffn_kernel(x_ref, gain_ref, wg_ref, wu_ref, wd_ref, *refs, round_weights, aliased, n_extra):
    refs = refs[1:] if aliased else refs
    extra_in, refs = refs[:n_extra], refs[n_extra:]
    o_ref, xn_ref = refs[0], refs[-1]
    rounding = list(zip(extra_in, refs[-1 - n_extra:-1]))
    if round_weights:
        rounding += list(zip((wg_ref, wu_ref, wd_ref), refs[1:4]))
        wg_ref, wu_ref, wd_ref = refs[1:4]

    def step(base_ref, normalize):
        for src, dst in rounding:
            dst[...] = src[...].astype(dst.dtype)
        if normalize:
            xn_ref[...] = _rms(x_ref[...], gain_ref[...]).astype(BF16)
        xn = xn_ref[...]
        tf = wg_ref.shape[1]
        piece = max(tf // 2, V7X_MXU_COLUMNS)
        starts = range(0, tf, piece)
        gu = [(_dot(xn, wg_ref[:, c:c + piece]), _dot(xn, wu_ref[:, c:c + piece]))
              for c in starts]
        acc = base_ref[...]
        for (g, u), c in zip(gu, starts):
            h = (g * jax.nn.sigmoid(g) * u * 0.5).astype(BF16)
            acc = acc + _dot(h, wd_ref[c:c + piece, :])
        o_ref[...] = acc

    first = pl.program_id(1) == 0
    pl.when(first)(lambda: step(x_ref, True))
    pl.when(jnp.logical_not(first))(lambda: step(o_ref, False))


def _ffn_specs(d, tm, tf, first_tile):
    row_block = pl.BlockSpec((tm, d), lambda i, j: (i + first_tile, 0))
    weight_specs = [
        pl.BlockSpec((d, tf), lambda i, j: (0, j)),
        pl.BlockSpec((d, tf), lambda i, j: (0, j)),
        pl.BlockSpec((tf, d), lambda i, j: (j, 0)),
    ]
    return row_block, [row_block, pl.BlockSpec((1, d), lambda i, j: (0, 0))] + weight_specs, weight_specs


def _ffn_head(x, gain, wg32, wu32, wd32):
    s, d = x.shape
    f = wg32.shape[1]
    tm, tf = FFN_ROWS, FFN_HEAD_COLS
    row_block, in_specs, weight_specs = _ffn_specs(d, tm, tf, 0)
    outs = pl.pallas_call(
        functools.partial(_ffn_kernel, round_weights=True, aliased=False, n_extra=0),
        name="ffn_head",
        grid=(FFN_HEAD_TILES, f // tf),
        in_specs=in_specs,
        out_specs=[row_block] + weight_specs,
        out_shape=[jax.ShapeDtypeStruct((s, d), F32)]
        + [jax.ShapeDtypeStruct(w.shape, BF16) for w in (wg32, wu32, wd32)],
        scratch_shapes=[pltpu.VMEM((tm, d), BF16)],
        compiler_params=_compiler_params(("arbitrary", "arbitrary")),
    )(x, gain, wg32, wu32, wd32)
    return outs[0], tuple(outs[1:])


def _flat_step_row_block(shape, grid):
    rows, cols = shape
    n_blocks = rows // BF16_SUBLANES
    assert rows % BF16_SUBLANES == 0 and n_blocks <= grid[0] * grid[1]
    return pl.BlockSpec((BF16_SUBLANES, cols),
                        lambda i, j: (jnp.minimum(i * grid[1] + j, n_blocks - 1), 0))


def _ffn(x, gain, wg, wu, wd, into=None, round_weights=()):
    s, d = x.shape
    f = wg.shape[1]
    tm, tf = FFN_ROWS, FFN_COLS
    first_tile = 0 if into is None else FFN_HEAD_TILES
    grid = (s // tm - first_tile, f // tf)
    row_block, in_specs, _ = _ffn_specs(d, tm, tf, first_tile)
    operands = [x, gain, wg, wu, wd]
    aliases = {}
    if into is not None:
        in_specs.append(pl.BlockSpec(memory_space=pl.ANY))
        aliases = {len(operands): 0}
        operands.append(into)
    extra_specs = [_flat_step_row_block(w.shape, grid) for w in round_weights]
    outs = pl.pallas_call(
        functools.partial(_ffn_kernel, round_weights=False, aliased=into is not None,
                          n_extra=len(round_weights)),
        name="ffn",
        grid=grid,
        in_specs=in_specs + extra_specs,
        out_specs=[row_block] + extra_specs,
        out_shape=[jax.ShapeDtypeStruct((s, d), F32)]
        + [jax.ShapeDtypeStruct(w.shape, BF16) for w in round_weights],
        input_output_aliases=aliases,
        scratch_shapes=[pltpu.VMEM((tm, d), BF16)],
        compiler_params=_compiler_params(
            ("arbitrary" if round_weights else "parallel", "arbitrary")),
    )(*operands, *round_weights)
    return outs[0], tuple(outs[1:])


def _head_norm_store(acc, gain, scale, out_ref):
    for h in range(N_HEADS):
        sl = slice(h * HEAD_DIM, (h + 1) * HEAD_DIM)
        out_ref[:, sl] = (_rms(acc[:, sl], gain) * scale).astype(out_ref.dtype)


def _build_attention_bias(i, table_ref, bias_ref):
    tq, tk = MIX_ROWS, MIX_KEYS
    r = lax.broadcasted_iota(jnp.int32, (tq, tk), 0)
    c = lax.broadcasted_iota(jnp.int32, (tq, tk), 1)
    q_chunk = r // CHUNK
    k_chunk = c // CHUNK
    first_key = (MIX_KEY_BLOCKS - 1 - i) * tq
    visible = (k_chunk >= q_chunk) & (k_chunk <= q_chunk + LEFT_CHUNKS) & (c >= first_key)
    for h in range(N_HEADS):
        rows = jnp.broadcast_to(table_ref[h:h + 1, :], (tq, BIAS_TABLE))
        rolled = pltpu.roll(rows, 0, 1, stride=1, stride_axis=0)
        bias_ref[h] = jnp.where(visible, rolled[:, :tk] * LOG2E, MASK_VALUE)


def _window_sums(z, w):
    span = 1
    while span < w:
        z = z + pltpu.roll(z, span, 0)
        span *= 2
    return z


def _mixer_kernel(h_ref, gain_ref, w_ref, qg_ref, kg_ref, table_ref, pw_ref, ps_ref, *refs):
    n_cast = (len(refs) - 8) // 2
    cast_in, (yp_ref, ya_ref, u_ref), cast_out = (refs[:n_cast], refs[n_cast:n_cast + 3],
                                                  refs[n_cast + 3:-5])
    q_ref, k_ring, v_ring, zs_ref, bias_ref = refs[-5:]
    nb = MIX_KEY_BLOCKS
    tq = MIX_ROWS
    width = ATTN_WIDTH
    i = pl.program_id(0)

    @pl.when(i < nb)
    def _():
        _build_attention_bias(i, table_ref, bias_ref)

    @pl.when(i == 0)
    def _():
        k_ring[...] = jnp.zeros_like(k_ring)
        v_ring[...] = jnp.zeros_like(v_ring)
        zs_ref[:POOL_HALO, :] = jnp.zeros((POOL_HALO, width), F32)

    @pl.when(i > 0)
    def _():
        zs_ref[:POOL_HALO, :] = zs_ref[tq:, :]

    def round_weights(part, parts=4):
        for src, dst in list(zip(cast_in, cast_out))[part::parts]:
            dst[...] = src[...].astype(dst.dtype)

    slot = lax.rem(i, nb)
    u_ref[...] = _rms(h_ref[...], gain_ref[...]).astype(BF16)
    zs_ref[POOL_HALO:, :] = _dot(u_ref[...], w_ref[:, 0:width])
    round_weights(0)
    _head_norm_store(_dot(u_ref[...], w_ref[:, width:2 * width]),
                     qg_ref[...], HEAD_DIM ** -0.5 * LOG2E, q_ref)
    round_weights(1)
    _head_norm_store(_dot(u_ref[...], w_ref[:, 2 * width:3 * width]),
                     kg_ref[...], 1.0, k_ring.at[slot])
    round_weights(2)
    v_ring[slot] = _dot(u_ref[...], w_ref[:, 3 * width:4 * width]).astype(BF16)
    round_weights(3)

    t = i * tq + lax.broadcasted_iota(jnp.int32, (tq, 1), 0)
    for g, w in enumerate(POOL_WINDOWS):
        cols = slice(g * POOL_GROUP_DIM, (g + 1) * POOL_GROUP_DIM)
        z = zs_ref[:, cols]
        inv_count = 1.0 / jnp.minimum(t + 1, w).astype(F32)
        diff = (_window_sums(z, w)[POOL_HALO:] * inv_count - z[POOL_HALO:]).astype(BF16)
        y = _dot(diff, pw_ref[g]) * ps_ref[:, cols]
        yp_ref[:, cols] = y.astype(yp_ref.dtype)

    slots = [lax.rem(i + 1 + b, nb) for b in range(nb)]

    def head_scores(h):
        hs = slice(h * HEAD_DIM, (h + 1) * HEAD_DIM)
        qh = q_ref[:, hs]
        return [lax.dot_general(qh, k_ring[slots[b], :, hs], (((1,), (1,)), ((), ())),
                                preferred_element_type=F32) + bias_ref[h, :, b * tq:(b + 1) * tq]
                for b in range(nb)]

    def head_probs(scores):
        m = scores[0].max(axis=-1, keepdims=True)
        for b in range(1, nb):
            m = jnp.maximum(m, scores[b].max(axis=-1, keepdims=True))
        probs = [jnp.exp2(sc - m) for sc in scores]
        denom = probs[0].sum(axis=-1, keepdims=True)
        for pr in probs[1:]:
            denom = denom + pr.sum(axis=-1, keepdims=True)
        return [pr.astype(BF16) for pr in probs], denom

    def head_output(h, probs, denom):
        hs = slice(h * HEAD_DIM, (h + 1) * HEAD_DIM)
        out = _dot(probs[0], v_ring[slots[0], :, hs])
        for b in range(1, nb):
            out = out + _dot(probs[b], v_ring[slots[b], :, hs])
        ya_ref[:, hs] = (out / denom).astype(ya_ref.dtype)

    ahead = MIX_HEADS_AHEAD
    pending = {h: head_scores(h) for h in range(ahead)}
    for h in range(N_HEADS):
        if h + ahead < N_HEADS:
            pending[h + ahead] = head_scores(h + ahead)
        head_output(h, *head_probs(pending.pop(h)))


def _bias_table(rel_bias):
    far = rel_bias[:, 2 * REL_CLIP:]
    head = jnp.broadcast_to(far, (rel_bias.shape[0], LEFT - REL_CLIP))
    body = rel_bias[:, :0:-1]
    tail = jnp.broadcast_to(far, (rel_bias.shape[0],
                                  BIAS_TABLE - (LEFT - REL_CLIP) - 2 * REL_CLIP))
    return jnp.concatenate([head, body, tail], axis=1).astype(F32)


def _row_block_per_step(shape, steps):
    rows, cols = shape
    rep = 1
    while rows % (steps // rep) or (rows // (steps // rep)) % BF16_SUBLANES:
        rep *= 2
    return pl.BlockSpec((rows // (steps // rep), cols), lambda i: (i // rep, 0))


def _mixer(h, gain, w_in, q_gain, k_gain, table, pool_w, pool_scale, round_weights=()):
    s, d = h.shape
    width = ATTN_WIDTH
    tq, nb = MIX_ROWS, MIX_KEY_BLOCKS
    steps = s // tq
    row_block = pl.BlockSpec((tq, width), lambda i: (i, 0))
    cast_specs = [_row_block_per_step(w.shape, steps) for w in round_weights]
    outs = pl.pallas_call(
        _mixer_kernel,
        name="mixer",
        grid=(steps,),
        in_specs=[
            pl.BlockSpec((tq, d), lambda i: (i, 0)),
            _resident((1, d)),
            _resident(w_in.shape),
            _resident((1, HEAD_DIM)),
            _resident((1, HEAD_DIM)),
            _resident(table.shape),
            _resident(pool_w.shape),
            _resident((1, width)),
        ] + cast_specs,
        out_specs=[row_block, row_block, pl.BlockSpec((tq, d), lambda i: (i, 0))] + cast_specs,
        out_shape=[
            jax.ShapeDtypeStruct((s, width), BF16),
            jax.ShapeDtypeStruct((s, width), BF16),
            jax.ShapeDtypeStruct((s, d), BF16),
        ] + [jax.ShapeDtypeStruct(w.shape, BF16) for w in round_weights],
        scratch_shapes=[
            pltpu.VMEM((tq, width), BF16),
            pltpu.VMEM((nb, tq, width), BF16),
            pltpu.VMEM((nb, tq, width), BF16),
            pltpu.VMEM((POOL_HALO + tq, width), F32),
            pltpu.VMEM((N_HEADS, tq, MIX_KEYS), F32),
        ],
        compiler_params=_compiler_params(("arbitrary",)),
    )(h, gain, w_in, q_gain, k_gain, table, pool_w, pool_scale, *round_weights)
    return outs[0], outs[1], outs[2], tuple(outs[3:])


def _merge_kernel(h_ref, u_ref, yp_ref, ya_ref, wg_ref, bg_ref, wa_ref, wb_ref, wo_ref, o_ref):
    d = h_ref.shape[1]
    g_pool = jax.nn.sigmoid(_dot(u_ref[...], wg_ref[:, :d]) + bg_ref[:, :d])
    merged = g_pool * _dot(yp_ref[...], wa_ref[...])
    g_attn = jax.nn.sigmoid(_dot(u_ref[...], wg_ref[:, d:]) + bg_ref[:, d:])
    merged = merged + g_attn * _dot(ya_ref[...], wb_ref[...])
    o_ref[...] = h_ref[...] + _dot(merged.astype(BF16), wo_ref[...])


def _merge(h, u, y_pool, y_attn, w_g, b_g, w_a, w_b, w_out):
    s, d = h.shape
    width = y_pool.shape[1]
    tm = MERGE_ROWS
    return pl.pallas_call(
        _merge_kernel,
        name="merge",
        grid=(s // tm,),
        in_specs=[
            pl.BlockSpec((tm, d), lambda i: (i, 0)),
            pl.BlockSpec((tm, d), lambda i: (i, 0)),
            pl.BlockSpec((tm, width), lambda i: (i, 0)),
            pl.BlockSpec((tm, width), lambda i: (i, 0)),
            _resident(w_g.shape),
            _resident(b_g.shape),
            _resident(w_a.shape),
            _resident(w_b.shape),
            _resident(w_out.shape),
        ],
        out_specs=pl.BlockSpec((tm, d), lambda i: (i, 0)),
        out_shape=jax.ShapeDtypeStruct((s, d), F32),
        compiler_params=_compiler_params(("parallel",)),
    )(h, u, y_pool, y_attn, w_g, b_g, w_a, w_b, w_out)


def _ple_kernel(h_ref, p_ref, gain_ref, wpg_ref, wple_ref, o_ref):
    t = _rms(h_ref[...], gain_ref[...]).astype(BF16)
    gate = jax.nn.sigmoid(_dot(t, wpg_ref[...]))
    emb = _dot(p_ref[...].astype(BF16), wple_ref[...])
    o_ref[...] = h_ref[...] + gate * emb


def _ple(h, p, gain, w_pg, w_ple):
    s, d = h.shape
    tm = PLE_ROWS
    return pl.pallas_call(
        _ple_kernel,
        name="ple",
        grid=(s // tm,),
        in_specs=[
            pl.BlockSpec((tm, d), lambda i: (i, 0)),
            pl.BlockSpec((tm, p.shape[1]), lambda i: (i, 0)),
            _resident((1, d)),
            _resident(w_pg.shape),
            _resident(w_ple.shape),
        ],
        out_specs=pl.BlockSpec((tm, d), lambda i: (i, 0)),
        out_shape=jax.ShapeDtypeStruct((s, d), F32),
        compiler_params=_compiler_params(("parallel",)),
    )(h, p, gain, w_pg, w_ple)


def kernel(x, p, ffn1_norm, ffn1_w_gate, ffn1_w_up, ffn1_w_down, mix_norm, w_in, pool_w, pool_scale, q_norm, k_norm, rel_bias, w_br_pool, w_br_attn, w_branch_gate, b_branch_gate, w_out, ffn2_norm, ffn2_w_gate, ffn2_w_up, ffn2_w_down, ple_norm, w_ple_gate, w_ple):
    batch, seq, d_model = x.shape
    assert batch == 1, "the row tiling treats the sequence as the only row axis"
    depth = p.shape[0]
    h = x.reshape(seq, d_model)
    for i in range(depth):
        row = lambda a: a[i].reshape(1, -1)
        mm = lambda a: a[i].astype(BF16)
        h_head, ffn1_weights = _ffn_head(h, row(ffn1_norm), ffn1_w_gate[i], ffn1_w_up[i],
                                         ffn1_w_down[i])
        h, (w_in16,) = _ffn(h, row(ffn1_norm), *ffn1_weights, into=h_head,
                            round_weights=(w_in[i],))
        later = (w_branch_gate[i], w_br_pool[i], w_br_attn[i], w_out[i], w_ple_gate[i],
                 ffn2_w_gate[i], ffn2_w_up[i], ffn2_w_down[i])
        y_pool, y_attn, u, later = _mixer(h, row(mix_norm), w_in16, row(q_norm), row(k_norm),
                                          _bias_table(rel_bias[i]), mm(pool_w), row(pool_scale),
                                          round_weights=later)
        w_g, w_a, w_b, w_o, w_pg = later[:5]
        h = _merge(h, u, y_pool, y_attn, w_g, row(b_branch_gate), w_a, w_b, w_o)
        h, _ = _ffn(h, row(ffn2_norm), *later[5:])
        h = _ple(h, p[i].reshape(seq, -1), row(ple_norm), w_pg, mm(w_ple))
    return h.reshape(batch, seq, d_model)
```

```python
import functools
import math

import jax
import jax.numpy as jnp
from jax import lax
from jax.experimental import pallas as pl
from jax.experimental.pallas import tpu as pltpu

F32 = jnp.float32
BF16 = jnp.bfloat16

EPS = 1e-6
MASK_VALUE = -1e30
LOG2E = math.log2(math.e)
CHUNK = 64
LEFT_CHUNKS = 8
LEFT = LEFT_CHUNKS * CHUNK
REL_CLIP = 256
N_HEADS = 8
HEAD_DIM = 128
ATTN_WIDTH = N_HEADS * HEAD_DIM
POOL_WINDOWS = (2, 4, 8, 16)
POOL_GROUP_DIM = 256
POOL_HALO = 16

V7X_VMEM_BYTES = 64 * 1024 * 1024
VMEM_LIMIT_BYTES = V7X_VMEM_BYTES - 6 * 1024 * 1024

FFN_ROWS = 1024
FFN_COLS = 512
FFN_HEAD_TILES = 1
FFN_HEAD_COLS = 256
BF16_SUBLANES = 16
V7X_MXU_COLUMNS = 256
MIX_ROWS = 256
MIX_KEY_BLOCKS = 1 + LEFT // MIX_ROWS
MIX_KEYS = MIX_KEY_BLOCKS * MIX_ROWS
MIX_HEADS_AHEAD = 1
BIAS_TABLE = 1024
MERGE_ROWS = 256
PLE_ROWS = 1024


def _compiler_params(semantics):
    return pltpu.CompilerParams(dimension_semantics=semantics,
                                vmem_limit_bytes=VMEM_LIMIT_BYTES)


def _resident(shape):
    return pl.BlockSpec(shape, lambda *_: (0,) * len(shape), pipeline_mode=pl.Buffered(1))


def _rms(xf, gain):
    return xf * lax.rsqrt(jnp.mean(xf * xf, axis=-1, keepdims=True) + EPS) * gain


def _dot(a, b):
    return jnp.dot(a, b, preferred_element_type=F32)


def _ffn_kernel(x_ref, gain_ref, wg_ref, wu_ref, wd_ref, *refs, round_weights, aliased, n_extra):
    refs = refs[1:] if aliased else refs
    extra_in, refs = refs[:n_extra], refs[n_extra:]
    o_ref, xn_ref = refs[0], refs[-1]
    rounding = list(zip(extra_in, refs[-1 - n_extra:-1]))
    if round_weights:
        rounding += list(zip((wg_ref, wu_ref, wd_ref), refs[1:4]))
        wg_ref, wu_ref, wd_ref = refs[1:4]

    def step(base_ref, normalize):
        for src, dst in rounding:
            dst[...] = src[...].astype(dst.dtype)
        if normalize:
            xn_ref[...] = _rms(x_ref[...], gain_ref[...]).astype(BF16)
        xn = xn_ref[...]
        tf = wg_ref.shape[1]
        piece = max(tf // 2, V7X_MXU_COLUMNS)
        starts = range(0, tf, piece)
        gu = [(_dot(xn, wg_ref[:, c:c + piece]), _dot(xn, wu_ref[:, c:c + piece]))
              for c in starts]
        acc = base_ref[...]
        for (g, u), c in zip(gu, starts):
            h = (g * jax.nn.sigmoid(g) * u * 0.5).astype(BF16)
            acc = acc + _dot(h, wd_ref[c:c + piece, :])
        o_ref[...] = acc

    first = pl.program_id(1) == 0
    pl.when(first)(lambda: step(x_ref, True))
    pl.when(jnp.logical_not(first))(lambda: step(o_ref, False))


def _ffn_specs(d, tm, tf, first_tile):
    row_block = pl.BlockSpec((tm, d), lambda i, j: (i + first_tile, 0))
    weight_specs = [
        pl.BlockSpec((d, tf), lambda i, j: (0, j)),
        pl.BlockSpec((d, tf), lambda i, j: (0, j)),
        pl.BlockSpec((tf, d), lambda i, j: (j, 0)),
    ]
    return row_block, [row_block, pl.BlockSpec((1, d), lambda i, j: (0, 0))] + weight_specs, weight_specs


def _ffn_head(x, gain, wg32, wu32, wd32):
    s, d = x.shape
    f = wg32.shape[1]
    tm, tf = FFN_ROWS, FFN_HEAD_COLS
    row_block, in_specs, weight_specs = _ffn_specs(d, tm, tf, 0)
    outs = pl.pallas_call(
        functools.partial(_ffn_kernel, round_weights=True, aliased=False, n_extra=0),
        name="ffn_head",
        grid=(FFN_HEAD_TILES, f // tf),
        in_specs=in_specs,
        out_specs=[row_block] + weight_specs,
        out_shape=[jax.ShapeDtypeStruct((s, d), F32)]
        + [jax.ShapeDtypeStruct(w.shape, BF16) for w in (wg32, wu32, wd32)],
        scratch_shapes=[pltpu.VMEM((tm, d), BF16)],
        compiler_params=_compiler_params(("arbitrary", "arbitrary")),
    )(x, gain, wg32, wu32, wd32)
    return outs[0], tuple(outs[1:])


def _flat_step_row_block(shape, grid):
    rows, cols = shape
    n_blocks = rows // BF16_SUBLANES
    assert rows % BF16_SUBLANES == 0 and n_blocks <= grid[0] * grid[1]
    return pl.BlockSpec((BF16_SUBLANES, cols),
                        lambda i, j: (jnp.minimum(i * grid[1] + j, n_blocks - 1), 0))


def _ffn(x, gain, wg, wu, wd, into=None, round_weights=()):
    s, d = x.shape
    f = wg.shape[1]
    tm, tf = FFN_ROWS, FFN_COLS
    first_tile = 0 if into is None else FFN_HEAD_TILES
    grid = (s // tm - first_tile, f // tf)
    row_block, in_specs, _ = _ffn_specs(d, tm, tf, first_tile)
    operands = [x, gain, wg, wu, wd]
    aliases = {}
    if into is not None:
        in_specs.append(pl.BlockSpec(memory_space=pl.ANY))
        aliases = {len(operands): 0}
        operands.append(into)
    extra_specs = [_flat_step_row_block(w.shape, grid) for w in round_weights]
    outs = pl.pallas_call(
        functools.partial(_ffn_kernel, round_weights=False, aliased=into is not None,
                          n_extra=len(round_weights)),
        name="ffn",
        grid=grid,
        in_specs=in_specs + extra_specs,
        out_specs=[row_block] + extra_specs,
        out_shape=[jax.ShapeDtypeStruct((s, d), F32)]
        + [jax.ShapeDtypeStruct(w.shape, BF16) for w in round_weights],
        input_output_aliases=aliases,
        scratch_shapes=[pltpu.VMEM((tm, d), BF16)],
        compiler_params=_compiler_params(
            ("arbitrary" if round_weights else "parallel", "arbitrary")),
    )(*operands, *round_weights)
    return outs[0], tuple(outs[1:])


def _head_norm_store(acc, gain, scale, out_ref):
    for h in range(N_HEADS):
        sl = slice(h * HEAD_DIM, (h + 1) * HEAD_DIM)
        out_ref[:, sl] = (_rms(acc[:, sl], gain) * scale).astype(out_ref.dtype)


def _build_attention_bias(i, table_ref, bias_ref):
    tq, tk = MIX_ROWS, MIX_KEYS
    r = lax.broadcasted_iota(jnp.int32, (tq, tk), 0)
    c = lax.broadcasted_iota(jnp.int32, (tq, tk), 1)
    q_chunk = r // CHUNK
    k_chunk = c // CHUNK
    first_key = (MIX_KEY_BLOCKS - 1 - i) * tq
    visible = (k_chunk >= q_chunk) & (k_chunk <= q_chunk + LEFT_CHUNKS) & (c >= first_key)
    for h in range(N_HEADS):
        rows = jnp.broadcast_to(table_ref[h:h + 1, :], (tq, BIAS_TABLE))
        rolled = pltpu.roll(rows, 0, 1, stride=1, stride_axis=0)
        bias_ref[h] = jnp.where(visible, rolled[:, :tk] * LOG2E, MASK_VALUE)


def _window_sums(z, w):
    span = 1
    while span < w:
        z = z + pltpu.roll(z, span, 0)
        span *= 2
    return z


def _mixer_kernel(h_ref, gain_ref, w_ref, qg_ref, kg_ref, table_ref, pw_ref, ps_ref, *refs):
    n_cast = (len(refs) - 8) // 2
    cast_in, (yp_ref, ya_ref, u_ref), cast_out = (refs[:n_cast], refs[n_cast:n_cast + 3],
                                                  refs[n_cast + 3:-5])
    q_ref, k_ring, v_ring, zs_ref, bias_ref = refs[-5:]
    nb = MIX_KEY_BLOCKS
    tq = MIX_ROWS
    width = ATTN_WIDTH
    i = pl.program_id(0)

    @pl.when(i < nb)
    def _():
        _build_attention_bias(i, table_ref, bias_ref)

    @pl.when(i == 0)
    def _():
        k_ring[...] = jnp.zeros_like(k_ring)
        v_ring[...] = jnp.zeros_like(v_ring)
        zs_ref[:POOL_HALO, :] = jnp.zeros((POOL_HALO, width), F32)

    @pl.when(i > 0)
    def _():
        zs_ref[:POOL_HALO, :] = zs_ref[tq:, :]

    def round_weights(part, parts=4):
        for src, dst in list(zip(cast_in, cast_out))[part::parts]:
            dst[...] = src[...].astype(dst.dtype)

    slot = lax.rem(i, nb)
    u_ref[...] = _rms(h_ref[...], gain_ref[...]).astype(BF16)
    zs_ref[POOL_HALO:, :] = _dot(u_ref[...], w_ref[:, 0:width])
    round_weights(0)
    _head_norm_store(_dot(u_ref[...], w_ref[:, width:2 * width]),
                     qg_ref[...], HEAD_DIM ** -0.5 * LOG2E, q_ref)
    round_weights(1)
    _head_norm_store(_dot(u_ref[...], w_ref[:, 2 * width:3 * width]),
                     kg_ref[...], 1.0, k_ring.at[slot])
    round_weights(2)
    v_ring[slot] = _dot(u_ref[...], w_ref[:, 3 * width:4 * width]).astype(BF16)
    round_weights(3)

    def pool():
        t = i * tq + lax.broadcasted_iota(jnp.int32, (tq, 1), 0)
        for g, w in enumerate(POOL_WINDOWS):
            cols = slice(g * POOL_GROUP_DIM, (g + 1) * POOL_GROUP_DIM)
            z = zs_ref[:, cols]
            inv_count = 1.0 / jnp.minimum(t + 1, w).astype(F32)
            diff = (_window_sums(z, w)[POOL_HALO:] * inv_count - z[POOL_HALO:]).astype(BF16)
            y = _dot(diff, pw_ref[g]) * ps_ref[:, cols]
            yp_ref[:, cols] = y.astype(yp_ref.dtype)

    slots = [lax.rem(i + 1 + b, nb) for b in range(nb)]

    def head_scores(h):
        hs = slice(h * HEAD_DIM, (h + 1) * HEAD_DIM)
        qh = q_ref[:, hs]
        return [lax.dot_general(qh, k_ring[slots[b], :, hs], (((1,), (1,)), ((), ())),
                                preferred_element_type=F32) + bias_ref[h, :, b * tq:(b + 1) * tq]
                for b in range(nb)]

    def head_probs(scores):
        m = scores[0].max(axis=-1, keepdims=True)
        for b in range(1, nb):
            m = jnp.maximum(m, scores[b].max(axis=-1, keepdims=True))
        return [jnp.exp2(sc - m).astype(BF16) for sc in scores]

    ones = jnp.ones((tq, HEAD_DIM), BF16)

    def head_output(h, probs):
        hs = slice(h * HEAD_DIM, (h + 1) * HEAD_DIM)
        out = _dot(probs[0], jnp.concatenate([v_ring[slots[0], :, hs], ones], axis=1))
        for b in range(1, nb):
            out = out + _dot(probs[b], jnp.concatenate([v_ring[slots[b], :, hs], ones], axis=1))
        ya_ref[:, hs] = (out[:, :HEAD_DIM] / out[:, HEAD_DIM:]).astype(ya_ref.dtype)

    ahead = MIX_HEADS_AHEAD
    pending = {h: head_scores(h) for h in range(ahead)}
    pool()
    for h in range(N_HEADS):
        if h + ahead < N_HEADS:
            pending[h + ahead] = head_scores(h + ahead)
        head_output(h, head_probs(pending.pop(h)))


def _bias_table(rel_bias):
    far = rel_bias[:, 2 * REL_CLIP:]
    head = jnp.broadcast_to(far, (rel_bias.shape[0], LEFT - REL_CLIP))
    body = rel_bias[:, :0:-1]
    tail = jnp.broadcast_to(far, (rel_bias.shape[0],
                                  BIAS_TABLE - (LEFT - REL_CLIP) - 2 * REL_CLIP))
    return jnp.concatenate([head, body, tail], axis=1).astype(F32)


def _row_block_per_step(shape, steps):
    rows, cols = shape
    rep = 1
    while rows % (steps // rep) or (rows // (steps // rep)) % BF16_SUBLANES:
        rep *= 2
    return pl.BlockSpec((rows // (steps // rep), cols), lambda i: (i // rep, 0))


def _mixer(h, gain, w_in, q_gain, k_gain, table, pool_w, pool_scale, round_weights=()):
    s, d = h.shape
    width = ATTN_WIDTH
    tq, nb = MIX_ROWS, MIX_KEY_BLOCKS
    steps = s // tq
    row_block = pl.BlockSpec((tq, width), lambda i: (i, 0))
    cast_specs = [_row_block_per_step(w.shape, steps) for w in round_weights]
    outs = pl.pallas_call(
        _mixer_kernel,
        name="mixer",
        grid=(steps,),
        in_specs=[
            pl.BlockSpec((tq, d), lambda i: (i, 0)),
            _resident((1, d)),
            _resident(w_in.shape),
            _resident((1, HEAD_DIM)),
            _resident((1, HEAD_DIM)),
            _resident(table.shape),
            _resident(pool_w.shape),
            _resident((1, width)),
        ] + cast_specs,
        out_specs=[row_block, row_block, pl.BlockSpec((tq, d), lambda i: (i, 0))] + cast_specs,
        out_shape=[
            jax.ShapeDtypeStruct((s, width), BF16),
            jax.ShapeDtypeStruct((s, width), BF16),
            jax.ShapeDtypeStruct((s, d), BF16),
        ] + [jax.ShapeDtypeStruct(w.shape, BF16) for w in round_weights],
        scratch_shapes=[
            pltpu.VMEM((tq, width), BF16),
            pltpu.VMEM((nb, tq, width), BF16),
            pltpu.VMEM((nb, tq, width), BF16),
            pltpu.VMEM((POOL_HALO + tq, width), F32),
            pltpu.VMEM((N_HEADS, tq, MIX_KEYS), F32),
        ],
        compiler_params=_compiler_params(("arbitrary",)),
    )(h, gain, w_in, q_gain, k_gain, table, pool_w, pool_scale, *round_weights)
    return outs[0], outs[1], outs[2], tuple(outs[3:])


def _merge_kernel(h_ref, u_ref, yp_ref, ya_ref, wg_ref, bg_ref, wa_ref, wb_ref, wo_ref, o_ref):
    d = h_ref.shape[1]
    g_pool = jax.nn.sigmoid(_dot(u_ref[...], wg_ref[:, :d]) + bg_ref[:, :d])
    merged = g_pool * _dot(yp_ref[...], wa_ref[...])
    g_attn = jax.nn.sigmoid(_dot(u_ref[...], wg_ref[:, d:]) + bg_ref[:, d:])
    merged = merged + g_attn * _dot(ya_ref[...], wb_ref[...])
    o_ref[...] = h_ref[...] + _dot(merged.astype(BF16), wo_ref[...])


def _merge(h, u, y_pool, y_attn, w_g, b_g, w_a, w_b, w_out):
    s, d = h.shape
    width = y_pool.shape[1]
    tm = MERGE_ROWS
    return pl.pallas_call(
        _merge_kernel,
        name="merge",
        grid=(s // tm,),
        in_specs=[
            pl.BlockSpec((tm, d), lambda i: (i, 0)),
            pl.BlockSpec((tm, d), lambda i: (i, 0)),
            pl.BlockSpec((tm, width), lambda i: (i, 0)),
            pl.BlockSpec((tm, width), lambda i: (i, 0)),
            _resident(w_g.shape),
            _resident(b_g.shape),
            _resident(w_a.shape),
            _resident(w_b.shape),
            _resident(w_out.shape),
        ],
        out_specs=pl.BlockSpec((tm, d), lambda i: (i, 0)),
        out_shape=jax.ShapeDtypeStruct((s, d), F32),
        compiler_params=_compiler_params(("parallel",)),
    )(h, u, y_pool, y_attn, w_g, b_g, w_a, w_b, w_out)


def _ple_kernel(h_ref, p_ref, gain_ref, wpg_ref, wple_ref, o_ref):
    t = _rms(h_ref[...], gain_ref[...]).astype(BF16)
    gate = jax.nn.sigmoid(_dot(t, wpg_ref[...]))
    emb = _dot(p_ref[...].astype(BF16), wple_ref[...])
    o_ref[...] = h_ref[...] + gate * emb


def _ple(h, p, gain, w_pg, w_ple):
    s, d = h.shape
    tm = PLE_ROWS
    return pl.pallas_call(
        _ple_kernel,
        name="ple",
        grid=(s // tm,),
        in_specs=[
            pl.BlockSpec((tm, d), lambda i: (i, 0)),
            pl.BlockSpec((tm, p.shape[1]), lambda i: (i, 0)),
            _resident((1, d)),
            _resident(w_pg.shape),
            _resident(w_ple.shape),
        ],
        out_specs=pl.BlockSpec((tm, d), lambda i: (i, 0)),
        out_shape=jax.ShapeDtypeStruct((s, d), F32),
        compiler_params=_compiler_params(("parallel",)),
    )(h, p, gain, w_pg, w_ple)


def kernel(x, p, ffn1_norm, ffn1_w_gate, ffn1_w_up, ffn1_w_down, mix_norm, w_in, pool_w, pool_scale, q_norm, k_norm, rel_bias, w_br_pool, w_br_attn, w_branch_gate, b_branch_gate, w_out, ffn2_norm, ffn2_w_gate, ffn2_w_up, ffn2_w_down, ple_norm, w_ple_gate, w_ple):
    batch, seq, d_model = x.shape
    assert batch == 1, "the row tiling treats the sequence as the only row axis"
    depth = p.shape[0]
    h = x.reshape(seq, d_model)
    for i in range(depth):
        row = lambda a: a[i].reshape(1, -1)
        mm = lambda a: a[i].astype(BF16)
        h_head, ffn1_weights = _ffn_head(h, row(ffn1_norm), ffn1_w_gate[i], ffn1_w_up[i],
                                         ffn1_w_down[i])
        h, (w_in16,) = _ffn(h, row(ffn1_norm), *ffn1_weights, into=h_head,
                            round_weights=(w_in[i],))
        later = (w_branch_gate[i], w_br_pool[i], w_br_attn[i], w_out[i], w_ple_gate[i],
                 ffn2_w_gate[i], ffn2_w_up[i], ffn2_w_down[i])
        y_pool, y_attn, u, later = _mixer(h, row(mix_norm), w_in16, row(q_norm), row(k_norm),
                                          _bias_table(rel_bias[i]), mm(pool_w), row(pool_scale),
                                          round_weights=later)
        w_g, w_a, w_b, w_o, w_pg = later[:5]
        h = _merge(h, u, y_pool, y_attn, w_g, row(b_branch_gate), w_a, w_b, w_o)
        h, _ = _ffn(h, row(ffn2_norm), *later[5:])
        h = _ple(h, p[i].reshape(seq, -1), row(ple_norm), w_pg, mm(w_ple))
    return h.reshape(batch, seq, d_model)
```

```python
import functools
import math

import jax
import jax.numpy as jnp
from jax import lax
from jax.experimental import pallas as pl
from jax.experimental.pallas import tpu as pltpu

F32 = jnp.float32
BF16 = jnp.bfloat16

EPS = 1e-6
MASK_VALUE = -1e30
LOG2E = math.log2(math.e)
CHUNK = 64
LEFT_CHUNKS = 8
LEFT = LEFT_CHUNKS * CHUNK
REL_CLIP = 256
N_HEADS = 8
HEAD_DIM = 128
ATTN_WIDTH = N_HEADS * HEAD_DIM
POOL_WINDOWS = (2, 4, 8, 16)
POOL_GROUP_DIM = 256
POOL_HALO = 16

V7X_VMEM_BYTES = 64 * 1024 * 1024
VMEM_LIMIT_BYTES = V7X_VMEM_BYTES - 6 * 1024 * 1024

FFN_ROWS = 1024
FFN_COLS = 512
FFN_HEAD_TILES = 1
FFN_HEAD_COLS = 256
BF16_SUBLANES = 16
V7X_MXU_COLUMNS = 256
MIX_ROWS = 256
MIX_KEY_BLOCKS = 1 + LEFT // MIX_ROWS
MIX_KEYS = MIX_KEY_BLOCKS * MIX_ROWS
MIX_HEADS_AHEAD = 1
BIAS_TABLE = 1024
MERGE_ROWS = 256
PLE_ROWS = 1024


def _compiler_params(semantics):
    return pltpu.CompilerParams(dimension_semantics=semantics,
                                vmem_limit_bytes=VMEM_LIMIT_BYTES)


def _resident(shape):
    return pl.BlockSpec(shape, lambda *_: (0,) * len(shape), pipeline_mode=pl.Buffered(1))


def _rms(xf, gain):
    return xf * lax.rsqrt(jnp.mean(xf * xf, axis=-1, keepdims=True) + EPS) * gain


def _dot(a, b):
    return jnp.dot(a, b, preferred_element_type=F32)


def _ffn_kernel(x_ref, gain_ref, wg_ref, wu_ref, wd_ref, *refs, round_weights, aliased, n_extra):
    refs = refs[1:] if aliased else refs
    extra_in, refs = refs[:n_extra], refs[n_extra:]
    o_ref, xn_ref = refs[0], refs[-1]
    rounding = list(zip(extra_in, refs[-1 - n_extra:-1]))
    if round_weights:
        rounding += list(zip((wg_ref, wu_ref, wd_ref), refs[1:4]))
        wg_ref, wu_ref, wd_ref = refs[1:4]

    def step(base_ref, normalize):
        for src, dst in rounding:
            dst[...] = src[...].astype(dst.dtype)
        if normalize:
            xn_ref[...] = _rms(x_ref[...], gain_ref[...]).astype(BF16)
        xn = xn_ref[...]
        tf = wg_ref.shape[1]
        piece = max(tf // 2, V7X_MXU_COLUMNS)
        starts = range(0, tf, piece)
        gu = [(_dot(xn, wg_ref[:, c:c + piece]), _dot(xn, wu_ref[:, c:c + piece]))
              for c in starts]
        h = jnp.concatenate([(g * jax.nn.sigmoid(g) * u * 0.5).astype(BF16) for g, u in gu], axis=1)
        o_ref[...] = base_ref[...] + _dot(h, wd_ref[...])

    first = pl.program_id(1) == 0
    pl.when(first)(lambda: step(x_ref, True))
    pl.when(jnp.logical_not(first))(lambda: step(o_ref, False))


def _ffn_specs(d, tm, tf, first_tile):
    row_block = pl.BlockSpec((tm, d), lambda i, j: (i + first_tile, 0))
    weight_specs = [
        pl.BlockSpec((d, tf), lambda i, j: (0, j)),
        pl.BlockSpec((d, tf), lambda i, j: (0, j)),
        pl.BlockSpec((tf, d), lambda i, j: (j, 0)),
    ]
    return row_block, [row_block, pl.BlockSpec((1, d), lambda i, j: (0, 0))] + weight_specs, weight_specs


def _ffn_head(x, gain, wg32, wu32, wd32):
    s, d = x.shape
    f = wg32.shape[1]
    tm, tf = FFN_ROWS, FFN_HEAD_COLS
    row_block, in_specs, weight_specs = _ffn_specs(d, tm, tf, 0)
    outs = pl.pallas_call(
        functools.partial(_ffn_kernel, round_weights=True, aliased=False, n_extra=0),
        name="ffn_head",
        grid=(FFN_HEAD_TILES, f // tf),
        in_specs=in_specs,
        out_specs=[row_block] + weight_specs,
        out_shape=[jax.ShapeDtypeStruct((s, d), F32)]
        + [jax.ShapeDtypeStruct(w.shape, BF16) for w in (wg32, wu32, wd32)],
        scratch_shapes=[pltpu.VMEM((tm, d), BF16)],
        compiler_params=_compiler_params(("arbitrary", "arbitrary")),
    )(x, gain, wg32, wu32, wd32)
    return outs[0], tuple(outs[1:])


def _flat_step_row_block(shape, grid):
    rows, cols = shape
    n_blocks = rows // BF16_SUBLANES
    assert rows % BF16_SUBLANES == 0 and n_blocks <= grid[0] * grid[1]
    return pl.BlockSpec((BF16_SUBLANES, cols),
                        lambda i, j: (jnp.minimum(i * grid[1] + j, n_blocks - 1), 0))


def _ffn(x, gain, wg, wu, wd, into=None, round_weights=()):
    s, d = x.shape
    f = wg.shape[1]
    tm, tf = FFN_ROWS, FFN_COLS
    first_tile = 0 if into is None else FFN_HEAD_TILES
    grid = (s // tm - first_tile, f // tf)
    row_block, in_specs, _ = _ffn_specs(d, tm, tf, first_tile)
    operands = [x, gain, wg, wu, wd]
    aliases = {}
    if into is not None:
        in_specs.append(pl.BlockSpec(memory_space=pl.ANY))
        aliases = {len(operands): 0}
        operands.append(into)
    extra_specs = [_flat_step_row_block(w.shape, grid) for w in round_weights]
    outs = pl.pallas_call(
        functools.partial(_ffn_kernel, round_weights=False, aliased=into is not None,
                          n_extra=len(round_weights)),
        name="ffn",
        grid=grid,
        in_specs=in_specs + extra_specs,
        out_specs=[row_block] + extra_specs,
        out_shape=[jax.ShapeDtypeStruct((s, d), F32)]
        + [jax.ShapeDtypeStruct(w.shape, BF16) for w in round_weights],
        input_output_aliases=aliases,
        scratch_shapes=[pltpu.VMEM((tm, d), BF16)],
        compiler_params=_compiler_params(
            ("arbitrary" if round_weights else "parallel", "arbitrary")),
    )(*operands, *round_weights)
    return outs[0], tuple(outs[1:])


def _head_norm_store(acc, gain, scale, out_ref):
    for h in range(N_HEADS):
        sl = slice(h * HEAD_DIM, (h + 1) * HEAD_DIM)
        out_ref[:, sl] = (_rms(acc[:, sl], gain) * scale).astype(out_ref.dtype)


def _build_attention_bias(i, table_ref, bias_ref):
    tq, tk = MIX_ROWS, MIX_KEYS
    r = lax.broadcasted_iota(jnp.int32, (tq, tk), 0)
    c = lax.broadcasted_iota(jnp.int32, (tq, tk), 1)
    q_chunk = r // CHUNK
    k_chunk = c // CHUNK
    first_key = (MIX_KEY_BLOCKS - 1 - i) * tq
    visible = (k_chunk >= q_chunk) & (k_chunk <= q_chunk + LEFT_CHUNKS) & (c >= first_key)
    for h in range(N_HEADS):
        rows = jnp.broadcast_to(table_ref[h:h + 1, :], (tq, BIAS_TABLE))
        rolled = pltpu.roll(rows, 0, 1, stride=1, stride_axis=0)
        bias_ref[h] = jnp.where(visible, rolled[:, :tk] * LOG2E, MASK_VALUE)


def _window_sums(z, w):
    span = 1
    while span < w:
        z = z + pltpu.roll(z, span, 0)
        span *= 2
    return z


def _mixer_kernel(h_ref, gain_ref, w_ref, qg_ref, kg_ref, table_ref, pw_ref, ps_ref, *refs):
    n_cast = (len(refs) - 8) // 2
    cast_in, (yp_ref, ya_ref, u_ref), cast_out = (refs[:n_cast], refs[n_cast:n_cast + 3],
                                                  refs[n_cast + 3:-5])
    q_ref, k_ring, v_ring, zs_ref, bias_ref = refs[-5:]
    nb = MIX_KEY_BLOCKS
    tq = MIX_ROWS
    width = ATTN_WIDTH
    i = pl.program_id(0)

    @pl.when(i < nb)
    def _():
        _build_attention_bias(i, table_ref, bias_ref)

    @pl.when(i == 0)
    def _():
        k_ring[...] = jnp.zeros_like(k_ring)
        v_ring[...] = jnp.zeros_like(v_ring)
        zs_ref[:POOL_HALO, :] = jnp.zeros((POOL_HALO, width), F32)

    @pl.when(i > 0)
    def _():
        zs_ref[:POOL_HALO, :] = zs_ref[tq:, :]

    def round_weights(part, parts=4):
        for src, dst in list(zip(cast_in, cast_out))[part::parts]:
            dst[...] = src[...].astype(dst.dtype)

    slot = lax.rem(i, nb)
    u_ref[...] = _rms(h_ref[...], gain_ref[...]).astype(BF16)
    zs_ref[POOL_HALO:, :] = _dot(u_ref[...], w_ref[:, 0:width])
    round_weights(0)
    _head_norm_store(_dot(u_ref[...], w_ref[:, width:2 * width]),
                     qg_ref[...], HEAD_DIM ** -0.5 * LOG2E, q_ref)
    round_weights(1)
    _head_norm_store(_dot(u_ref[...], w_ref[:, 2 * width:3 * width]),
                     kg_ref[...], 1.0, k_ring.at[slot])
    round_weights(2)
    v_ring[slot] = _dot(u_ref[...], w_ref[:, 3 * width:4 * width]).astype(BF16)
    round_weights(3)

    def pool():
        t = i * tq + lax.broadcasted_iota(jnp.int32, (tq, 1), 0)
        for g, w in enumerate(POOL_WINDOWS):
            cols = slice(g * POOL_GROUP_DIM, (g + 1) * POOL_GROUP_DIM)
            z = zs_ref[:, cols]
            inv_count = 1.0 / jnp.minimum(t + 1, w).astype(F32)
            diff = (_window_sums(z, w)[POOL_HALO:] * inv_count - z[POOL_HALO:]).astype(BF16)
            y = _dot(diff, pw_ref[g]) * ps_ref[:, cols]
            yp_ref[:, cols] = y.astype(yp_ref.dtype)

    slots = [lax.rem(i + 1 + b, nb) for b in range(nb)]

    def head_scores(h):
        hs = slice(h * HEAD_DIM, (h + 1) * HEAD_DIM)
        qh = q_ref[:, hs]
        return [lax.dot_general(qh, k_ring[slots[b], :, hs], (((1,), (1,)), ((), ())),
                                preferred_element_type=F32) + bias_ref[h, :, b * tq:(b + 1) * tq]
                for b in range(nb)]

    def head_probs(scores):
        m = scores[0].max(axis=-1, keepdims=True)
        for b in range(1, nb):
            m = jnp.maximum(m, scores[b].max(axis=-1, keepdims=True))
        return [jnp.exp2(sc - m).astype(BF16) for sc in scores]

    ones = jnp.ones((tq, HEAD_DIM), BF16)

    def head_output(h, probs):
        hs = slice(h * HEAD_DIM, (h + 1) * HEAD_DIM)
        out = _dot(probs[0], jnp.concatenate([v_ring[slots[0], :, hs], ones], axis=1))
        for b in range(1, nb):
            out = out + _dot(probs[b], jnp.concatenate([v_ring[slots[b], :, hs], ones], axis=1))
        ya_ref[:, hs] = (out[:, :HEAD_DIM] / out[:, HEAD_DIM:]).astype(ya_ref.dtype)

    ahead = MIX_HEADS_AHEAD
    pending = {h: head_scores(h) for h in range(ahead)}
    pool()
    for h in range(N_HEADS):
        if h + ahead < N_HEADS:
            pending[h + ahead] = head_scores(h + ahead)
        head_output(h, head_probs(pending.pop(h)))


def _bias_table(rel_bias):
    far = rel_bias[:, 2 * REL_CLIP:]
    head = jnp.broadcast_to(far, (rel_bias.shape[0], LEFT - REL_CLIP))
    body = rel_bias[:, :0:-1]
    tail = jnp.broadcast_to(far, (rel_bias.shape[0],
                                  BIAS_TABLE - (LEFT - REL_CLIP) - 2 * REL_CLIP))
    return jnp.concatenate([head, body, tail], axis=1).astype(F32)


def _row_block_per_step(shape, steps):
    rows, cols = shape
    rep = 1
    while rows % (steps // rep) or (rows // (steps // rep)) % BF16_SUBLANES:
        rep *= 2
    return pl.BlockSpec((rows // (steps // rep), cols), lambda i: (i // rep, 0))


def _mixer(h, gain, w_in, q_gain, k_gain, table, pool_w, pool_scale, round_weights=()):
    s, d = h.shape
    width = ATTN_WIDTH
    tq, nb = MIX_ROWS, MIX_KEY_BLOCKS
    steps = s // tq
    row_block = pl.BlockSpec((tq, width), lambda i: (i, 0))
    cast_specs = [_row_block_per_step(w.shape, steps) for w in round_weights]
    outs = pl.pallas_call(
        _mixer_kernel,
        name="mixer",
        grid=(steps,),
        in_specs=[
            pl.BlockSpec((tq, d), lambda i: (i, 0)),
            _resident((1, d)),
            _resident(w_in.shape),
            _resident((1, HEAD_DIM)),
            _resident((1, HEAD_DIM)),
            _resident(table.shape),
            _resident(pool_w.shape),
            _resident((1, width)),
        ] + cast_specs,
        out_specs=[row_block, row_block, pl.BlockSpec((tq, d), lambda i: (i, 0))] + cast_specs,
        out_shape=[
            jax.ShapeDtypeStruct((s, width), BF16),
            jax.ShapeDtypeStruct((s, width), BF16),
            jax.ShapeDtypeStruct((s, d), BF16),
        ] + [jax.ShapeDtypeStruct(w.shape, BF16) for w in round_weights],
        scratch_shapes=[
            pltpu.VMEM((tq, width), BF16),
            pltpu.VMEM((nb, tq, width), BF16),
            pltpu.VMEM((nb, tq, width), BF16),
            pltpu.VMEM((POOL_HALO + tq, width), F32),
            pltpu.VMEM((N_HEADS, tq, MIX_KEYS), F32),
        ],
        compiler_params=_compiler_params(("arbitrary",)),
    )(h, gain, w_in, q_gain, k_gain, table, pool_w, pool_scale, *round_weights)
    return outs[0], outs[1], outs[2], tuple(outs[3:])


def _merge_kernel(h_ref, u_ref, yp_ref, ya_ref, wg_ref, bg_ref, wa_ref, wb_ref, wo_ref, o_ref):
    d = h_ref.shape[1]
    g_pool = jax.nn.sigmoid(_dot(u_ref[...], wg_ref[:, :d]) + bg_ref[:, :d])
    merged = g_pool * _dot(yp_ref[...], wa_ref[...])
    g_attn = jax.nn.sigmoid(_dot(u_ref[...], wg_ref[:, d:]) + bg_ref[:, d:])
    merged = merged + g_attn * _dot(ya_ref[...], wb_ref[...])
    o_ref[...] = h_ref[...] + _dot(merged.astype(BF16), wo_ref[...])


def _merge(h, u, y_pool, y_attn, w_g, b_g, w_a, w_b, w_out):
    s, d = h.shape
    width = y_pool.shape[1]
    tm = MERGE_ROWS
    return pl.pallas_call(
        _merge_kernel,
        name="merge",
        grid=(s // tm,),
        in_specs=[
            pl.BlockSpec((tm, d), lambda i: (i, 0)),
            pl.BlockSpec((tm, d), lambda i: (i, 0)),
            pl.BlockSpec((tm, width), lambda i: (i, 0)),
            pl.BlockSpec((tm, width), lambda i: (i, 0)),
            _resident(w_g.shape),
            _resident(b_g.shape),
            _resident(w_a.shape),
            _resident(w_b.shape),
            _resident(w_out.shape),
        ],
        out_specs=pl.BlockSpec((tm, d), lambda i: (i, 0)),
        out_shape=jax.ShapeDtypeStruct((s, d), F32),
        compiler_params=_compiler_params(("parallel",)),
    )(h, u, y_pool, y_attn, w_g, b_g, w_a, w_b, w_out)


def _ple_kernel(h_ref, p_ref, gain_ref, wpg_ref, wple_ref, o_ref):
    t = _rms(h_ref[...], gain_ref[...]).astype(BF16)
    gate = jax.nn.sigmoid(_dot(t, wpg_ref[...]))
    emb = _dot(p_ref[...].astype(BF16), wple_ref[...])
    o_ref[...] = h_ref[...] + gate * emb


def _ple(h, p, gain, w_pg, w_ple):
    s, d = h.shape
    tm = PLE_ROWS
    return pl.pallas_call(
        _ple_kernel,
        name="ple",
        grid=(s // tm,),
        in_specs=[
            pl.BlockSpec((tm, d), lambda i: (i, 0)),
            pl.BlockSpec((tm, p.shape[1]), lambda i: (i, 0)),
            _resident((1, d)),
            _resident(w_pg.shape),
            _resident(w_ple.shape),
        ],
        out_specs=pl.BlockSpec((tm, d), lambda i: (i, 0)),
        out_shape=jax.ShapeDtypeStruct((s, d), F32),
        compiler_params=_compiler_params(("parallel",)),
    )(h, p, gain, w_pg, w_ple)


def kernel(x, p, ffn1_norm, ffn1_w_gate, ffn1_w_up, ffn1_w_down, mix_norm, w_in, pool_w, pool_scale, q_norm, k_norm, rel_bias, w_br_pool, w_br_attn, w_branch_gate, b_branch_gate, w_out, ffn2_norm, ffn2_w_gate, ffn2_w_up, ffn2_w_down, ple_norm, w_ple_gate, w_ple):
    batch, seq, d_model = x.shape
    assert batch == 1, "the row tiling treats the sequence as the only row axis"
    depth = p.shape[0]
    h = x.reshape(seq, d_model)
    for i in range(depth):
        row = lambda a: a[i].reshape(1, -1)
        mm = lambda a: a[i].astype(BF16)
        h_head, ffn1_weights = _ffn_head(h, row(ffn1_norm), ffn1_w_gate[i], ffn1_w_up[i],
                                         ffn1_w_down[i])
        h, (w_in16,) = _ffn(h, row(ffn1_norm), *ffn1_weights, into=h_head,
                            round_weights=(w_in[i],))
        later = (w_branch_gate[i], w_br_pool[i], w_br_attn[i], w_out[i], w_ple_gate[i],
                 ffn2_w_gate[i], ffn2_w_up[i], ffn2_w_down[i])
        y_pool, y_attn, u, later = _mixer(h, row(mix_norm), w_in16, row(q_norm), row(k_norm),
                                          _bias_table(rel_bias[i]), mm(pool_w), row(pool_scale),
                                          round_weights=later)
        w_g, w_a, w_b, w_o, w_pg = later[:5]
        h = _merge(h, u, y_pool, y_attn, w_g, row(b_branch_gate), w_a, w_b, w_o)
        h, _ = _ffn(h, row(ffn2_norm), *later[5:])
        h = _ple(h, p[i].reshape(seq, -1), row(ple_norm), w_pg, mm(w_ple))
    return h.reshape(batch, seq, d_model)
```

```python
import functools
import math

import jax
import jax.numpy as jnp
from jax import lax
from jax.experimental import pallas as pl
from jax.experimental.pallas import tpu as pltpu

F32 = jnp.float32
BF16 = jnp.bfloat16

EPS = 1e-6
MASK_VALUE = -1e30
LOG2E = math.log2(math.e)
CHUNK = 64
LEFT_CHUNKS = 8
LEFT = LEFT_CHUNKS * CHUNK
REL_CLIP = 256
N_HEADS = 8
HEAD_DIM = 128
ATTN_WIDTH = N_HEADS * HEAD_DIM
POOL_WINDOWS = (2, 4, 8, 16)
POOL_GROUP_DIM = 256
POOL_HALO = 16

V7X_VMEM_BYTES = 64 * 1024 * 1024
VMEM_LIMIT_BYTES = V7X_VMEM_BYTES - 6 * 1024 * 1024

FFN_ROWS = 1024
FFN_COLS = 512
FFN_HEAD_TILES = 1
FFN_HEAD_COLS = 256
FFN_HEAD_RING = 3
FFN_HEAD_RING_PRIORITY = 1
BF16_SUBLANES = 16
V7X_MXU_COLUMNS = 256
MIX_ROWS = 256
MIX_KEY_BLOCKS = 1 + LEFT // MIX_ROWS
MIX_KEYS = MIX_KEY_BLOCKS * MIX_ROWS
MIX_HEADS_AHEAD = 1
BIAS_TABLE = 1024
MERGE_ROWS = 256
PLE_ROWS = 1024


def _compiler_params(semantics):
    return pltpu.CompilerParams(dimension_semantics=semantics,
                                vmem_limit_bytes=VMEM_LIMIT_BYTES)


def _resident(shape):
    return pl.BlockSpec(shape, lambda *_: (0,) * len(shape), pipeline_mode=pl.Buffered(1))


def _rms(xf, gain):
    return xf * lax.rsqrt(jnp.mean(xf * xf, axis=-1, keepdims=True) + EPS) * gain


def _dot(a, b):
    return jnp.dot(a, b, preferred_element_type=F32)


def _ffn_kernel(x_ref, gain_ref, wg_ref, wu_ref, wd_ref, *refs, round_weights, aliased, n_extra):
    refs = refs[1:] if aliased else refs
    extra_in, refs = refs[:n_extra], refs[n_extra:]
    if round_weights:
        *refs, xn_ref, ring_g, ring_u, ring_d, sem = refs
    else:
        *refs, xn_ref = refs
    o_ref = refs[0]
    rounding = list(zip(extra_in, refs[len(refs) - n_extra:]))
    if round_weights:
        j, nj = pl.program_id(1), pl.num_programs(1)
        tf = ring_g.shape[2]
        ahead = FFN_HEAD_RING - 1

        def copies(tile):
            slot = lax.rem(tile, FFN_HEAD_RING)
            cols = pl.ds(pl.multiple_of(tile * tf, tf), tf)
            return [pltpu.make_async_copy(wg_ref.at[:, cols], ring_g.at[slot], sem.at[0, slot]),
                    pltpu.make_async_copy(wu_ref.at[:, cols], ring_u.at[slot], sem.at[1, slot]),
                    pltpu.make_async_copy(wd_ref.at[cols, :], ring_d.at[slot], sem.at[2, slot])]

        @pl.when(j == 0)
        def _():
            for tile in range(ahead):
                for copy in copies(tile):
                    copy.start(priority=FFN_HEAD_RING_PRIORITY)

        @pl.when(j + ahead < nj)
        def _():
            for copy in copies(j + ahead):
                copy.start(priority=FFN_HEAD_RING_PRIORITY)

        for copy in copies(j):
            copy.wait()
        slot = lax.rem(j, FFN_HEAD_RING)
        rounding += list(zip((ring_g.at[slot], ring_u.at[slot], ring_d.at[slot]), refs[1:4]))
        wg_ref, wu_ref, wd_ref = refs[1:4]

    def step(base_ref, normalize):
        for src, dst in rounding:
            dst[...] = src[...].astype(dst.dtype)
        if normalize:
            xn_ref[...] = _rms(x_ref[...], gain_ref[...]).astype(BF16)
        xn = xn_ref[...]
        tf = wg_ref.shape[1]
        piece = max(tf // 2, V7X_MXU_COLUMNS)
        starts = range(0, tf, piece)
        gu = [(_dot(xn, wg_ref[:, c:c + piece]), _dot(xn, wu_ref[:, c:c + piece]))
              for c in starts]
        h = jnp.concatenate([(g * jax.nn.sigmoid(g) * u * 0.5).astype(BF16) for g, u in gu], axis=1)
        o_ref[...] = base_ref[...] + _dot(h, wd_ref[...])

    first = pl.program_id(1) == 0
    pl.when(first)(lambda: step(x_ref, True))
    pl.when(jnp.logical_not(first))(lambda: step(o_ref, False))


def _ffn_specs(d, tm, tf, first_tile):
    row_block = pl.BlockSpec((tm, d), lambda i, j: (i + first_tile, 0))
    weight_specs = [
        pl.BlockSpec((d, tf), lambda i, j: (0, j)),
        pl.BlockSpec((d, tf), lambda i, j: (0, j)),
        pl.BlockSpec((tf, d), lambda i, j: (j, 0)),
    ]
    return row_block, [row_block, pl.BlockSpec((1, d), lambda i, j: (0, 0))] + weight_specs, weight_specs


def _ffn_head(x, gain, wg32, wu32, wd32):
    s, d = x.shape
    f = wg32.shape[1]
    tm, tf = FFN_ROWS, FFN_HEAD_COLS
    assert FFN_HEAD_TILES == 1, "the weight ring walks the hidden tiles once"
    row_block, in_specs, weight_specs = _ffn_specs(d, tm, tf, 0)
    in_hbm = pl.BlockSpec(memory_space=pl.ANY)
    outs = pl.pallas_call(
        functools.partial(_ffn_kernel, round_weights=True, aliased=False, n_extra=0),
        name="ffn_head",
        grid=(FFN_HEAD_TILES, f // tf),
        in_specs=in_specs[:2] + [in_hbm, in_hbm, in_hbm],
        out_specs=[row_block] + weight_specs,
        out_shape=[jax.ShapeDtypeStruct((s, d), F32)]
        + [jax.ShapeDtypeStruct(w.shape, BF16) for w in (wg32, wu32, wd32)],
        scratch_shapes=[
            pltpu.VMEM((tm, d), BF16),
            pltpu.VMEM((FFN_HEAD_RING, d, tf), F32),
            pltpu.VMEM((FFN_HEAD_RING, d, tf), F32),
            pltpu.VMEM((FFN_HEAD_RING, tf, d), F32),
            pltpu.SemaphoreType.DMA((3, FFN_HEAD_RING)),
        ],
        compiler_params=_compiler_params(("arbitrary", "arbitrary")),
    )(x, gain, wg32, wu32, wd32)
    return outs[0], tuple(outs[1:])


def _flat_step_row_block(shape, grid):
    rows, cols = shape
    n_blocks = rows // BF16_SUBLANES
    assert rows % BF16_SUBLANES == 0 and n_blocks <= grid[0] * grid[1]
    return pl.BlockSpec((BF16_SUBLANES, cols),
                        lambda i, j: (jnp.minimum(i * grid[1] + j, n_blocks - 1), 0))


def _ffn(x, gain, wg, wu, wd, into=None, round_weights=()):
    s, d = x.shape
    f = wg.shape[1]
    tm, tf = FFN_ROWS, FFN_COLS
    first_tile = 0 if into is None else FFN_HEAD_TILES
    grid = (s // tm - first_tile, f // tf)
    row_block, in_specs, _ = _ffn_specs(d, tm, tf, first_tile)
    operands = [x, gain, wg, wu, wd]
    aliases = {}
    if into is not None:
        in_specs.append(pl.BlockSpec(memory_space=pl.ANY))
        aliases = {len(operands): 0}
        operands.append(into)
    extra_specs = [_flat_step_row_block(w.shape, grid) for w in round_weights]
    outs = pl.pallas_call(
        functools.partial(_ffn_kernel, round_weights=False, aliased=into is not None,
                          n_extra=len(round_weights)),
        name="ffn",
        grid=grid,
        in_specs=in_specs + extra_specs,
        out_specs=[row_block] + extra_specs,
        out_shape=[jax.ShapeDtypeStruct((s, d), F32)]
        + [jax.ShapeDtypeStruct(w.shape, BF16) for w in round_weights],
        input_output_aliases=aliases,
        scratch_shapes=[pltpu.VMEM((tm, d), BF16)],
        compiler_params=_compiler_params(
            ("arbitrary" if round_weights else "parallel", "arbitrary")),
    )(*operands, *round_weights)
    return outs[0], tuple(outs[1:])


def _head_norm_store(acc, gain, scale, out_ref):
    for h in range(N_HEADS):
        sl = slice(h * HEAD_DIM, (h + 1) * HEAD_DIM)
        out_ref[:, sl] = (_rms(acc[:, sl], gain) * scale).astype(out_ref.dtype)


def _build_attention_bias(i, table_ref, bias_ref):
    tq, tk = MIX_ROWS, MIX_KEYS
    r = lax.broadcasted_iota(jnp.int32, (tq, tk), 0)
    c = lax.broadcasted_iota(jnp.int32, (tq, tk), 1)
    q_chunk = r // CHUNK
    k_chunk = c // CHUNK
    first_key = (MIX_KEY_BLOCKS - 1 - i) * tq
    visible = (k_chunk >= q_chunk) & (k_chunk <= q_chunk + LEFT_CHUNKS) & (c >= first_key)
    for h in range(N_HEADS):
        rows = jnp.broadcast_to(table_ref[h:h + 1, :], (tq, BIAS_TABLE))
        rolled = pltpu.roll(rows, 0, 1, stride=1, stride_axis=0)
        bias_ref[h] = jnp.where(visible, rolled[:, :tk] * LOG2E, MASK_VALUE)


def _window_sums(z, w):
    span = 1
    while span < w:
        z = z + pltpu.roll(z, span, 0)
        span *= 2
    return z


def _mixer_kernel(h_ref, gain_ref, w_ref, qg_ref, kg_ref, table_ref, pw_ref, ps_ref, *refs):
    n_cast = (len(refs) - 8) // 2
    cast_in, (yp_ref, ya_ref, u_ref), cast_out = (refs[:n_cast], refs[n_cast:n_cast + 3],
                                                  refs[n_cast + 3:-5])
    q_ref, k_ring, v_ring, zs_ref, bias_ref = refs[-5:]
    nb = MIX_KEY_BLOCKS
    tq = MIX_ROWS
    width = ATTN_WIDTH
    i = pl.program_id(0)

    @pl.when(i < nb)
    def _():
        _build_attention_bias(i, table_ref, bias_ref)

    @pl.when(i == 0)
    def _():
        k_ring[...] = jnp.zeros_like(k_ring)
        v_ring[...] = jnp.zeros_like(v_ring)
        zs_ref[:POOL_HALO, :] = jnp.zeros((POOL_HALO, width), F32)

    @pl.when(i > 0)
    def _():
        zs_ref[:POOL_HALO, :] = zs_ref[tq:, :]

    def round_weights(part, parts=4):
        for src, dst in list(zip(cast_in, cast_out))[part::parts]:
            dst[...] = src[...].astype(dst.dtype)

    slot = lax.rem(i, nb)
    u_ref[...] = _rms(h_ref[...], gain_ref[...]).astype(BF16)
    zs_ref[POOL_HALO:, :] = _dot(u_ref[...], w_ref[:, 0:width])
    round_weights(0)
    _head_norm_store(_dot(u_ref[...], w_ref[:, width:2 * width]),
                     qg_ref[...], HEAD_DIM ** -0.5 * LOG2E, q_ref)
    round_weights(1)
    _head_norm_store(_dot(u_ref[...], w_ref[:, 2 * width:3 * width]),
                     kg_ref[...], 1.0, k_ring.at[slot])
    round_weights(2)
    v_ring[slot] = _dot(u_ref[...], w_ref[:, 3 * width:4 * width]).astype(BF16)
    round_weights(3)

    def pool():
        t = i * tq + lax.broadcasted_iota(jnp.int32, (tq, 1), 0)
        for g, w in enumerate(POOL_WINDOWS):
            cols = slice(g * POOL_GROUP_DIM, (g + 1) * POOL_GROUP_DIM)
            z = zs_ref[:, cols]
            inv_count = 1.0 / jnp.minimum(t + 1, w).astype(F32)
            diff = (_window_sums(z, w)[POOL_HALO:] * inv_count - z[POOL_HALO:]).astype(BF16)
            y = _dot(diff, pw_ref[g]) * ps_ref[:, cols]
            yp_ref[:, cols] = y.astype(yp_ref.dtype)

    slots = [lax.rem(i + 1 + b, nb) for b in range(nb)]

    def head_scores(h):
        hs = slice(h * HEAD_DIM, (h + 1) * HEAD_DIM)
        qh = q_ref[:, hs]
        return [lax.dot_general(qh, k_ring[slots[b], :, hs], (((1,), (1,)), ((), ())),
                                preferred_element_type=F32) + bias_ref[h, :, b * tq:(b + 1) * tq]
                for b in range(nb)]

    def head_probs(scores):
        m = scores[0].max(axis=-1, keepdims=True)
        for b in range(1, nb):
            m = jnp.maximum(m, scores[b].max(axis=-1, keepdims=True))
        return [jnp.exp2(sc - m).astype(BF16) for sc in scores]

    ones = jnp.ones((tq, HEAD_DIM), BF16)

    def head_output(h, probs):
        hs = slice(h * HEAD_DIM, (h + 1) * HEAD_DIM)
        out = _dot(probs[0], jnp.concatenate([v_ring[slots[0], :, hs], ones], axis=1))
        for b in range(1, nb):
            out = out + _dot(probs[b], jnp.concatenate([v_ring[slots[b], :, hs], ones], axis=1))
        ya_ref[:, hs] = (out[:, :HEAD_DIM] / out[:, HEAD_DIM:]).astype(ya_ref.dtype)

    ahead = MIX_HEADS_AHEAD
    pending = {h: head_scores(h) for h in range(ahead)}
    pool()
    for h in range(N_HEADS):
        if h + ahead < N_HEADS:
            pending[h + ahead] = head_scores(h + ahead)
        head_output(h, head_probs(pending.pop(h)))


def _bias_table(rel_bias):
    far = rel_bias[:, 2 * REL_CLIP:]
    head = jnp.broadcast_to(far, (rel_bias.shape[0], LEFT - REL_CLIP))
    body = rel_bias[:, :0:-1]
    tail = jnp.broadcast_to(far, (rel_bias.shape[0],
                                  BIAS_TABLE - (LEFT - REL_CLIP) - 2 * REL_CLIP))
    return jnp.concatenate([head, body, tail], axis=1).astype(F32)


def _row_block_per_step(shape, steps):
    rows, cols = shape
    rep = 1
    while rows % (steps // rep) or (rows // (steps // rep)) % BF16_SUBLANES:
        rep *= 2
    return pl.BlockSpec((rows // (steps // rep), cols), lambda i: (i // rep, 0))


def _mixer(h, gain, w_in, q_gain, k_gain, table, pool_w, pool_scale, round_weights=()):
    s, d = h.shape
    width = ATTN_WIDTH
    tq, nb = MIX_ROWS, MIX_KEY_BLOCKS
    steps = s // tq
    row_block = pl.BlockSpec((tq, width), lambda i: (i, 0))
    cast_specs = [_row_block_per_step(w.shape, steps) for w in round_weights]
    outs = pl.pallas_call(
        _mixer_kernel,
        name="mixer",
        grid=(steps,),
        in_specs=[
            pl.BlockSpec((tq, d), lambda i: (i, 0)),
            _resident((1, d)),
            _resident(w_in.shape),
            _resident((1, HEAD_DIM)),
            _resident((1, HEAD_DIM)),
            _resident(table.shape),
            _resident(pool_w.shape),
            _resident((1, width)),
        ] + cast_specs,
        out_specs=[row_block, row_block, pl.BlockSpec((tq, d), lambda i: (i, 0))] + cast_specs,
        out_shape=[
            jax.ShapeDtypeStruct((s, width), BF16),
            jax.ShapeDtypeStruct((s, width), BF16),
            jax.ShapeDtypeStruct((s, d), BF16),
        ] + [jax.ShapeDtypeStruct(w.shape, BF16) for w in round_weights],
        scratch_shapes=[
            pltpu.VMEM((tq, width), BF16),
            pltpu.VMEM((nb, tq, width), BF16),
            pltpu.VMEM((nb, tq, width), BF16),
            pltpu.VMEM((POOL_HALO + tq, width), F32),
            pltpu.VMEM((N_HEADS, tq, MIX_KEYS), F32),
        ],
        compiler_params=_compiler_params(("arbitrary",)),
    )(h, gain, w_in, q_gain, k_gain, table, pool_w, pool_scale, *round_weights)
    return outs[0], outs[1], outs[2], tuple(outs[3:])


def _merge_kernel(h_ref, u_ref, yp_ref, ya_ref, wg_ref, bg_ref, wa_ref, wb_ref, wo_ref, o_ref):
    d = h_ref.shape[1]
    g_pool = jax.nn.sigmoid(_dot(u_ref[...], wg_ref[:, :d]) + bg_ref[:, :d])
    merged = g_pool * _dot(yp_ref[...], wa_ref[...])
    g_attn = jax.nn.sigmoid(_dot(u_ref[...], wg_ref[:, d:]) + bg_ref[:, d:])
    merged = merged + g_attn * _dot(ya_ref[...], wb_ref[...])
    o_ref[...] = h_ref[...] + _dot(merged.astype(BF16), wo_ref[...])


def _merge(h, u, y_pool, y_attn, w_g, b_g, w_a, w_b, w_out):
    s, d = h.shape
    width = y_pool.shape[1]
    tm = MERGE_ROWS
    return pl.pallas_call(
        _merge_kernel,
        name="merge",
        grid=(s // tm,),
        in_specs=[
            pl.BlockSpec((tm, d), lambda i: (i, 0)),
            pl.BlockSpec((tm, d), lambda i: (i, 0)),
            pl.BlockSpec((tm, width), lambda i: (i, 0)),
            pl.BlockSpec((tm, width), lambda i: (i, 0)),
            _resident(w_g.shape),
            _resident(b_g.shape),
            _resident(w_a.shape),
            _resident(w_b.shape),
            _resident(w_out.shape),
        ],
        out_specs=pl.BlockSpec((tm, d), lambda i: (i, 0)),
        out_shape=jax.ShapeDtypeStruct((s, d), F32),
        compiler_params=_compiler_params(("parallel",)),
    )(h, u, y_pool, y_attn, w_g, b_g, w_a, w_b, w_out)


def _ple_kernel(h_ref, p_ref, gain_ref, wpg_ref, wple_ref, o_ref):
    t = _rms(h_ref[...], gain_ref[...]).astype(BF16)
    gate = jax.nn.sigmoid(_dot(t, wpg_ref[...]))
    emb = _dot(p_ref[...].astype(BF16), wple_ref[...])
    o_ref[...] = h_ref[...] + gate * emb


def _ple(h, p, gain, w_pg, w_ple):
    s, d = h.shape
    tm = PLE_ROWS
    return pl.pallas_call(
        _ple_kernel,
        name="ple",
        grid=(s // tm,),
        in_specs=[
            pl.BlockSpec((tm, d), lambda i: (i, 0)),
            pl.BlockSpec((tm, p.shape[1]), lambda i: (i, 0)),
            _resident((1, d)),
            _resident(w_pg.shape),
            _resident(w_ple.shape),
        ],
        out_specs=pl.BlockSpec((tm, d), lambda i: (i, 0)),
        out_shape=jax.ShapeDtypeStruct((s, d), F32),
        compiler_params=_compiler_params(("parallel",)),
    )(h, p, gain, w_pg, w_ple)


def kernel(x, p, ffn1_norm, ffn1_w_gate, ffn1_w_up, ffn1_w_down, mix_norm, w_in, pool_w, pool_scale, q_norm, k_norm, rel_bias, w_br_pool, w_br_attn, w_branch_gate, b_branch_gate, w_out, ffn2_norm, ffn2_w_gate, ffn2_w_up, ffn2_w_down, ple_norm, w_ple_gate, w_ple):
    batch, seq, d_model = x.shape
    assert batch == 1, "the row tiling treats the sequence as the only row axis"
    depth = p.shape[0]
    h = x.reshape(seq, d_model)
    for i in range(depth):
        row = lambda a: a[i].reshape(1, -1)
        mm = lambda a: a[i].astype(BF16)
        h_head, ffn1_weights = _ffn_head(h, row(ffn1_norm), ffn1_w_gate[i], ffn1_w_up[i],
                                         ffn1_w_down[i])
        h, (w_in16,) = _ffn(h, row(ffn1_norm), *ffn1_weights, into=h_head,
                            round_weights=(w_in[i],))
        later = (w_branch_gate[i], w_br_pool[i], w_br_attn[i], w_out[i], w_ple_gate[i],
                 ffn2_w_gate[i], ffn2_w_up[i], ffn2_w_down[i])
        y_pool, y_attn, u, later = _mixer(h, row(mix_norm), w_in16, row(q_norm), row(k_norm),
                                          _bias_table(rel_bias[i]), mm(pool_w), row(pool_scale),
                                          round_weights=later)
        w_g, w_a, w_b, w_o, w_pg = later[:5]
        h = _merge(h, u, y_pool, y_attn, w_g, row(b_branch_gate), w_a, w_b, w_o)
        h, _ = _ffn(h, row(ffn2_norm), *later[5:])
        h = _ple(h, p[i].reshape(seq, -1), row(ple_norm), w_pg, mm(w_ple))
    return h.reshape(batch, seq, d_model)
```

```python
import functools
import math

import jax
import jax.numpy as jnp
from jax import lax
from jax.experimental import pallas as pl
from jax.experimental.pallas import tpu as pltpu

F32 = jnp.float32
BF16 = jnp.bfloat16

EPS = 1e-6
MASK_VALUE = -1e30
LOG2E = math.log2(math.e)
CHUNK = 64
LEFT_CHUNKS = 8
LEFT = LEFT_CHUNKS * CHUNK
REL_CLIP = 256
N_HEADS = 8
HEAD_DIM = 128
ATTN_WIDTH = N_HEADS * HEAD_DIM
POOL_WINDOWS = (2, 4, 8, 16)
POOL_GROUP_DIM = 256
POOL_HALO = 16

V7X_VMEM_BYTES = 64 * 1024 * 1024
VMEM_LIMIT_BYTES = V7X_VMEM_BYTES - 6 * 1024 * 1024

FFN_ROWS = 1024
FFN_COLS = 512
FFN_HEAD_TILES = 1
FFN_HEAD_COLS = 256
FFN_HEAD_RING = 4
BF16_SUBLANES = 16
V7X_MXU_COLUMNS = 256
MIX_ROWS = 256
MIX_KEY_BLOCKS = 1 + LEFT // MIX_ROWS
MIX_KEYS = MIX_KEY_BLOCKS * MIX_ROWS
MIX_HEADS_AHEAD = 1
BIAS_TABLE = 1024
MERGE_ROWS = 256
PLE_ROWS = 1024


def _compiler_params(semantics):
    return pltpu.CompilerParams(dimension_semantics=semantics,
                                vmem_limit_bytes=VMEM_LIMIT_BYTES)


def _resident(shape):
    return pl.BlockSpec(shape, lambda *_: (0,) * len(shape), pipeline_mode=pl.Buffered(1))


def _rms(xf, gain):
    return xf * lax.rsqrt(jnp.mean(xf * xf, axis=-1, keepdims=True) + EPS) * gain


def _dot(a, b):
    return jnp.dot(a, b, preferred_element_type=F32)


def _ffn_kernel(x_ref, gain_ref, wg_ref, wu_ref, wd_ref, *refs, round_weights, aliased, n_extra):
    refs = refs[1:] if aliased else refs
    extra_in, refs = refs[:n_extra], refs[n_extra:]
    if round_weights:
        *refs, xn_ref, ring_g, ring_u, ring_d, sem = refs
    else:
        *refs, xn_ref = refs
    o_ref = refs[0]
    rounding = list(zip(extra_in, refs[len(refs) - n_extra:]))
    if round_weights:
        j, nj = pl.program_id(1), pl.num_programs(1)
        tf = ring_g.shape[2]
        ahead = FFN_HEAD_RING - 1

        def copies(tile):
            slot = lax.rem(tile, FFN_HEAD_RING)
            cols = pl.ds(pl.multiple_of(tile * tf, tf), tf)
            return [pltpu.make_async_copy(wg_ref.at[:, cols], ring_g.at[slot], sem.at[0, slot]),
                    pltpu.make_async_copy(wu_ref.at[:, cols], ring_u.at[slot], sem.at[1, slot]),
                    pltpu.make_async_copy(wd_ref.at[cols, :], ring_d.at[slot], sem.at[2, slot])]

        @pl.when(j == 0)
        def _():
            for tile in range(ahead):
                for copy in copies(tile):
                    copy.start()

        @pl.when(j + ahead < nj)
        def _():
            for copy in copies(j + ahead):
                copy.start()

        for copy in copies(j):
            copy.wait()
        slot = lax.rem(j, FFN_HEAD_RING)
        rounding += list(zip((ring_g.at[slot], ring_u.at[slot], ring_d.at[slot]), refs[1:4]))
        wg_ref, wu_ref, wd_ref = refs[1:4]

    def step(base_ref, normalize):
        for src, dst in rounding:
            dst[...] = src[...].astype(dst.dtype)
        if normalize:
            xn_ref[...] = _rms(x_ref[...], gain_ref[...]).astype(BF16)
        xn = xn_ref[...]
        tf = wg_ref.shape[1]
        piece = max(tf // 2, V7X_MXU_COLUMNS)
        starts = range(0, tf, piece)
        gu = [(_dot(xn, wg_ref[:, c:c + piece]), _dot(xn, wu_ref[:, c:c + piece]))
              for c in starts]
        h = jnp.concatenate([(g * jax.nn.sigmoid(g) * u * 0.5).astype(BF16) for g, u in gu], axis=1)
        o_ref[...] = base_ref[...] + _dot(h, wd_ref[...])

    first = pl.program_id(1) == 0
    pl.when(first)(lambda: step(x_ref, True))
    pl.when(jnp.logical_not(first))(lambda: step(o_ref, False))


def _ffn_specs(d, tm, tf, first_tile):
    row_block = pl.BlockSpec((tm, d), lambda i, j: (i + first_tile, 0))
    weight_specs = [
        pl.BlockSpec((d, tf), lambda i, j: (0, j)),
        pl.BlockSpec((d, tf), lambda i, j: (0, j)),
        pl.BlockSpec((tf, d), lambda i, j: (j, 0)),
    ]
    return row_block, [row_block, pl.BlockSpec((1, d), lambda i, j: (0, 0))] + weight_specs, weight_specs


def _ffn_head(x, gain, wg32, wu32, wd32):
    s, d = x.shape
    f = wg32.shape[1]
    tm, tf = FFN_ROWS, FFN_HEAD_COLS
    assert FFN_HEAD_TILES == 1, "the weight ring walks the hidden tiles once"
    row_block, in_specs, weight_specs = _ffn_specs(d, tm, tf, 0)
    in_hbm = pl.BlockSpec(memory_space=pl.ANY)
    outs = pl.pallas_call(
        functools.partial(_ffn_kernel, round_weights=True, aliased=False, n_extra=0),
        name="ffn_head",
        grid=(FFN_HEAD_TILES, f // tf),
        in_specs=in_specs[:2] + [in_hbm, in_hbm, in_hbm],
        out_specs=[row_block] + weight_specs,
        out_shape=[jax.ShapeDtypeStruct((s, d), F32)]
        + [jax.ShapeDtypeStruct(w.shape, BF16) for w in (wg32, wu32, wd32)],
        scratch_shapes=[
            pltpu.VMEM((tm, d), BF16),
            pltpu.VMEM((FFN_HEAD_RING, d, tf), F32),
            pltpu.VMEM((FFN_HEAD_RING, d, tf), F32),
            pltpu.VMEM((FFN_HEAD_RING, tf, d), F32),
            pltpu.SemaphoreType.DMA((3, FFN_HEAD_RING)),
        ],
        compiler_params=_compiler_params(("arbitrary", "arbitrary")),
    )(x, gain, wg32, wu32, wd32)
    return outs[0], tuple(outs[1:])


def _flat_step_row_block(shape, grid):
    rows, cols = shape
    n_blocks = rows // BF16_SUBLANES
    assert rows % BF16_SUBLANES == 0 and n_blocks <= grid[0] * grid[1]
    return pl.BlockSpec((BF16_SUBLANES, cols),
                        lambda i, j: (jnp.minimum(i * grid[1] + j, n_blocks - 1), 0))


def _ffn(x, gain, wg, wu, wd, into=None, round_weights=()):
    s, d = x.shape
    f = wg.shape[1]
    tm, tf = FFN_ROWS, FFN_COLS
    first_tile = 0 if into is None else FFN_HEAD_TILES
    grid = (s // tm - first_tile, f // tf)
    row_block, in_specs, _ = _ffn_specs(d, tm, tf, first_tile)
    operands = [x, gain, wg, wu, wd]
    aliases = {}
    if into is not None:
        in_specs.append(pl.BlockSpec(memory_space=pl.ANY))
        aliases = {len(operands): 0}
        operands.append(into)
    extra_specs = [_flat_step_row_block(w.shape, grid) for w in round_weights]
    outs = pl.pallas_call(
        functools.partial(_ffn_kernel, round_weights=False, aliased=into is not None,
                          n_extra=len(round_weights)),
        name="ffn",
        grid=grid,
        in_specs=in_specs + extra_specs,
        out_specs=[row_block] + extra_specs,
        out_shape=[jax.ShapeDtypeStruct((s, d), F32)]
        + [jax.ShapeDtypeStruct(w.shape, BF16) for w in round_weights],
        input_output_aliases=aliases,
        scratch_shapes=[pltpu.VMEM((tm, d), BF16)],
        compiler_params=_compiler_params(
            ("arbitrary" if round_weights else "parallel", "arbitrary")),
    )(*operands, *round_weights)
    return outs[0], tuple(outs[1:])


def _head_norm_store(acc, gain, scale, out_ref):
    for h in range(N_HEADS):
        sl = slice(h * HEAD_DIM, (h + 1) * HEAD_DIM)
        out_ref[:, sl] = (_rms(acc[:, sl], gain) * scale).astype(out_ref.dtype)


def _build_attention_bias(i, table_ref, bias_ref):
    tq, tk = MIX_ROWS, MIX_KEYS
    r = lax.broadcasted_iota(jnp.int32, (tq, tk), 0)
    c = lax.broadcasted_iota(jnp.int32, (tq, tk), 1)
    q_chunk = r // CHUNK
    k_chunk = c // CHUNK
    first_key = (MIX_KEY_BLOCKS - 1 - i) * tq
    visible = (k_chunk >= q_chunk) & (k_chunk <= q_chunk + LEFT_CHUNKS) & (c >= first_key)
    for h in range(N_HEADS):
        rows = jnp.broadcast_to(table_ref[h:h + 1, :], (tq, BIAS_TABLE))
        rolled = pltpu.roll(rows, 0, 1, stride=1, stride_axis=0)
        bias_ref[h] = jnp.where(visible, rolled[:, :tk] * LOG2E, MASK_VALUE)


def _window_sums(z, w):
    span = 1
    while span < w:
        z = z + pltpu.roll(z, span, 0)
        span *= 2
    return z


def _mixer_kernel(h_ref, gain_ref, w_ref, qg_ref, kg_ref, table_ref, pw_ref, ps_ref, *refs):
    n_cast = (len(refs) - 8) // 2
    cast_in, (yp_ref, ya_ref, u_ref), cast_out = (refs[:n_cast], refs[n_cast:n_cast + 3],
                                                  refs[n_cast + 3:-5])
    q_ref, k_ring, v_ring, zs_ref, bias_ref = refs[-5:]
    nb = MIX_KEY_BLOCKS
    tq = MIX_ROWS
    width = ATTN_WIDTH
    i = pl.program_id(0)

    @pl.when(i < nb)
    def _():
        _build_attention_bias(i, table_ref, bias_ref)

    @pl.when(i == 0)
    def _():
        k_ring[...] = jnp.zeros_like(k_ring)
        v_ring[...] = jnp.zeros_like(v_ring)
        zs_ref[:POOL_HALO, :] = jnp.zeros((POOL_HALO, width), F32)

    @pl.when(i > 0)
    def _():
        zs_ref[:POOL_HALO, :] = zs_ref[tq:, :]

    def round_weights(part, parts=4):
        for src, dst in list(zip(cast_in, cast_out))[part::parts]:
            dst[...] = src[...].astype(dst.dtype)

    slot = lax.rem(i, nb)
    u_ref[...] = _rms(h_ref[...], gain_ref[...]).astype(BF16)
    zs_ref[POOL_HALO:, :] = _dot(u_ref[...], w_ref[:, 0:width])
    round_weights(0)
    _head_norm_store(_dot(u_ref[...], w_ref[:, width:2 * width]),
                     qg_ref[...], HEAD_DIM ** -0.5 * LOG2E, q_ref)
    round_weights(1)
    _head_norm_store(_dot(u_ref[...], w_ref[:, 2 * width:3 * width]),
                     kg_ref[...], 1.0, k_ring.at[slot])
    round_weights(2)
    v_ring[slot] = _dot(u_ref[...], w_ref[:, 3 * width:4 * width]).astype(BF16)
    round_weights(3)

    def pool():
        t = i * tq + lax.broadcasted_iota(jnp.int32, (tq, 1), 0)
        for g, w in enumerate(POOL_WINDOWS):
            cols = slice(g * POOL_GROUP_DIM, (g + 1) * POOL_GROUP_DIM)
            z = zs_ref[:, cols]
            inv_count = 1.0 / jnp.minimum(t + 1, w).astype(F32)
            diff = (_window_sums(z, w)[POOL_HALO:] * inv_count - z[POOL_HALO:]).astype(BF16)
            y = _dot(diff, pw_ref[g]) * ps_ref[:, cols]
            yp_ref[:, cols] = y.astype(yp_ref.dtype)

    slots = [lax.rem(i + 1 + b, nb) for b in range(nb)]

    def head_scores(h):
        hs = slice(h * HEAD_DIM, (h + 1) * HEAD_DIM)
        qh = q_ref[:, hs]
        return [lax.dot_general(qh, k_ring[slots[b], :, hs], (((1,), (1,)), ((), ())),
                                preferred_element_type=F32) + bias_ref[h, :, b * tq:(b + 1) * tq]
                for b in range(nb)]

    def head_probs(scores):
        m = scores[0].max(axis=-1, keepdims=True)
        for b in range(1, nb):
            m = jnp.maximum(m, scores[b].max(axis=-1, keepdims=True))
        return [jnp.exp2(sc - m).astype(BF16) for sc in scores]

    ones = jnp.ones((tq, HEAD_DIM), BF16)

    def head_output(h, probs):
        hs = slice(h * HEAD_DIM, (h + 1) * HEAD_DIM)
        out = _dot(probs[0], jnp.concatenate([v_ring[slots[0], :, hs], ones], axis=1))
        for b in range(1, nb):
            out = out + _dot(probs[b], jnp.concatenate([v_ring[slots[b], :, hs], ones], axis=1))
        ya_ref[:, hs] = (out[:, :HEAD_DIM] / out[:, HEAD_DIM:]).astype(ya_ref.dtype)

    ahead = MIX_HEADS_AHEAD
    pending = {h: head_scores(h) for h in range(ahead)}
    pool()
    for h in range(N_HEADS):
        if h + ahead < N_HEADS:
            pending[h + ahead] = head_scores(h + ahead)
        head_output(h, head_probs(pending.pop(h)))


def _bias_table(rel_bias):
    far = rel_bias[:, 2 * REL_CLIP:]
    head = jnp.broadcast_to(far, (rel_bias.shape[0], LEFT - REL_CLIP))
    body = rel_bias[:, :0:-1]
    tail = jnp.broadcast_to(far, (rel_bias.shape[0],
                                  BIAS_TABLE - (LEFT - REL_CLIP) - 2 * REL_CLIP))
    return jnp.concatenate([head, body, tail], axis=1).astype(F32)


def _row_block_per_step(shape, steps):
    rows, cols = shape
    rep = 1
    while rows % (steps // rep) or (rows // (steps // rep)) % BF16_SUBLANES:
        rep *= 2
    return pl.BlockSpec((rows // (steps // rep), cols), lambda i: (i // rep, 0))


def _mixer(h, gain, w_in, q_gain, k_gain, table, pool_w, pool_scale, round_weights=()):
    s, d = h.shape
    width = ATTN_WIDTH
    tq, nb = MIX_ROWS, MIX_KEY_BLOCKS
    steps = s // tq
    row_block = pl.BlockSpec((tq, width), lambda i: (i, 0))
    cast_specs = [_row_block_per_step(w.shape, steps) for w in round_weights]
    outs = pl.pallas_call(
        _mixer_kernel,
        name="mixer",
        grid=(steps,),
        in_specs=[
            pl.BlockSpec((tq, d), lambda i: (i, 0)),
            _resident((1, d)),
            _resident(w_in.shape),
            _resident((1, HEAD_DIM)),
            _resident((1, HEAD_DIM)),
            _resident(table.shape),
            _resident(pool_w.shape),
            _resident((1, width)),
        ] + cast_specs,
        out_specs=[row_block, row_block, pl.BlockSpec((tq, d), lambda i: (i, 0))] + cast_specs,
        out_shape=[
            jax.ShapeDtypeStruct((s, width), BF16),
            jax.ShapeDtypeStruct((s, width), BF16),
            jax.ShapeDtypeStruct((s, d), BF16),
        ] + [jax.ShapeDtypeStruct(w.shape, BF16) for w in round_weights],
        scratch_shapes=[
            pltpu.VMEM((tq, width), BF16),
            pltpu.VMEM((nb, tq, width), BF16),
            pltpu.VMEM((nb, tq, width), BF16),
            pltpu.VMEM((POOL_HALO + tq, width), F32),
            pltpu.VMEM((N_HEADS, tq, MIX_KEYS), F32),
        ],
        compiler_params=_compiler_params(("arbitrary",)),
    )(h, gain, w_in, q_gain, k_gain, table, pool_w, pool_scale, *round_weights)
    return outs[0], outs[1], outs[2], tuple(outs[3:])


def _merge_kernel(h_ref, u_ref, yp_ref, ya_ref, wg_ref, bg_ref, wa_ref, wb_ref, wo_ref, o_ref):
    d = h_ref.shape[1]
    g_pool = jax.nn.sigmoid(_dot(u_ref[...], wg_ref[:, :d]) + bg_ref[:, :d])
    merged = g_pool * _dot(yp_ref[...], wa_ref[...])
    g_attn = jax.nn.sigmoid(_dot(u_ref[...], wg_ref[:, d:]) + bg_ref[:, d:])
    merged = merged + g_attn * _dot(ya_ref[...], wb_ref[...])
    o_ref[...] = h_ref[...] + _dot(merged.astype(BF16), wo_ref[...])


def _merge(h, u, y_pool, y_attn, w_g, b_g, w_a, w_b, w_out):
    s, d = h.shape
    width = y_pool.shape[1]
    tm = MERGE_ROWS
    return pl.pallas_call(
        _merge_kernel,
        name="merge",
        grid=(s // tm,),
        in_specs=[
            pl.BlockSpec((tm, d), lambda i: (i, 0)),
            pl.BlockSpec((tm, d), lambda i: (i, 0)),
            pl.BlockSpec((tm, width), lambda i: (i, 0)),
            pl.BlockSpec((tm, width), lambda i: (i, 0)),
            _resident(w_g.shape),
            _resident(b_g.shape),
            _resident(w_a.shape),
            _resident(w_b.shape),
            _resident(w_out.shape),
        ],
        out_specs=pl.BlockSpec((tm, d), lambda i: (i, 0)),
        out_shape=jax.ShapeDtypeStruct((s, d), F32),
        compiler_params=_compiler_params(("parallel",)),
    )(h, u, y_pool, y_attn, w_g, b_g, w_a, w_b, w_out)


def _ple_kernel(h_ref, p_ref, gain_ref, wpg_ref, wple_ref, o_ref):
    t = _rms(h_ref[...], gain_ref[...]).astype(BF16)
    gate = jax.nn.sigmoid(_dot(t, wpg_ref[...]))
    emb = _dot(p_ref[...].astype(BF16), wple_ref[...])
    o_ref[...] = h_ref[...] + gate * emb


def _ple(h, p, gain, w_pg, w_ple):
    s, d = h.shape
    tm = PLE_ROWS
    return pl.pallas_call(
        _ple_kernel,
        name="ple",
        grid=(s // tm,),
        in_specs=[
            pl.BlockSpec((tm, d), lambda i: (i, 0)),
            pl.BlockSpec((tm, p.shape[1]), lambda i: (i, 0)),
            _resident((1, d)),
            _resident(w_pg.shape),
            _resident(w_ple.shape),
        ],
        out_specs=pl.BlockSpec((tm, d), lambda i: (i, 0)),
        out_shape=jax.ShapeDtypeStruct((s, d), F32),
        compiler_params=_compiler_params(("parallel",)),
    )(h, p, gain, w_pg, w_ple)


def kernel(x, p, ffn1_norm, ffn1_w_gate, ffn1_w_up, ffn1_w_down, mix_norm, w_in, pool_w, pool_scale, q_norm, k_norm, rel_bias, w_br_pool, w_br_attn, w_branch_gate, b_branch_gate, w_out, ffn2_norm, ffn2_w_gate, ffn2_w_up, ffn2_w_down, ple_norm, w_ple_gate, w_ple):
    batch, seq, d_model = x.shape
    assert batch == 1, "the row tiling treats the sequence as the only row axis"
    depth = p.shape[0]
    h = x.reshape(seq, d_model)
    for i in range(depth):
        row = lambda a: a[i].reshape(1, -1)
        mm = lambda a: a[i].astype(BF16)
        h_head, ffn1_weights = _ffn_head(h, row(ffn1_norm), ffn1_w_gate[i], ffn1_w_up[i],
                                         ffn1_w_down[i])
        h, (w_in16,) = _ffn(h, row(ffn1_norm), *ffn1_weights, into=h_head,
                            round_weights=(w_in[i],))
        later = (w_branch_gate[i], w_br_pool[i], w_br_attn[i], w_out[i], w_ple_gate[i],
                 ffn2_w_gate[i], ffn2_w_up[i], ffn2_w_down[i])
        y_pool, y_attn, u, later = _mixer(h, row(mix_norm), w_in16, row(q_norm), row(k_norm),
                                          _bias_table(rel_bias[i]), mm(pool_w), row(pool_scale),
                                          round_weights=later)
        w_g, w_a, w_b, w_o, w_pg = later[:5]
        h = _merge(h, u, y_pool, y_attn, w_g, row(b_branch_gate), w_a, w_b, w_o)
        h, _ = _ffn(h, row(ffn2_norm), *later[5:])
        h = _ple(h, p[i].reshape(seq, -1), row(ple_norm), w_pg, mm(w_ple))
    return h.reshape(batch, seq, d_model)
```
